```python
import math, functools
import jax, jax.numpy as jnp
from jax import lax
import numpy as np

D_MODEL = 1024
BATCH = 16
SEQ = 2048
DEPTH = 1
DEC_BATCH = 32
DEC_SEQ = 1
PAST_LEN = 16384
PAGE_SIZE = 128

A_HEADS = 8
A_HEAD_DIM = 64
A_WIDTH = A_HEADS * A_HEAD_DIM
DECAY_RANK = 64
ICLR_RANK = 64
GATE_RANK = 128
GN_EPS = 64e-5
B_HEADS = 8
B_HEAD_DIM = 64
B_WIDTH = B_HEADS * B_HEAD_DIM
Q_BLOCK = 128
SHIFT_COLS = 3 * A_WIDTH + DECAY_RANK + ICLR_RANK + GATE_RANK
FOX_COLS = 3 * B_WIDTH + B_HEADS
GATE_COLS = 2 * D_MODEL
IN_COLS = SHIFT_COLS + FOX_COLS + GATE_COLS
A_SPLITS = [A_WIDTH, 2 * A_WIDTH, 3 * A_WIDTH, 3 * A_WIDTH + DECAY_RANK, 3 * A_WIDTH + DECAY_RANK + ICLR_RANK]
PEER_HEADS = 8
PEER_N_KEYS = 128
PEER_N_EXPERTS = PEER_N_KEYS * PEER_N_KEYS
PEER_TOPK = 16
PEER_QDIM = 256
PEER_HALF = PEER_QDIM // 2
PEER_BLOCK = 128
DN_ALPHA = (2.0 * DEPTH) ** 0.25
DN_BETA = (8.0 * DEPTH) ** -0.25
LN_EPS = 1e-5

kernel_name = 'rwkv7_fox_peer_hybrid_step'


def layer_norm(x, g, b):
    xf = x.astype(jnp.float32)
    mu = jnp.mean(xf, axis=-1, keepdims=True)
    var = jnp.mean(jnp.square(xf - mu), axis=-1, keepdims=True)
    return ((xf - mu) * lax.rsqrt(var + LN_EPS) * g + b).astype(x.dtype)


def ada_modulation(c, w_ada, b_ada):
    m = jax.nn.silu(c) @ w_ada + b_ada
    return jnp.split(m[:, None, :], 6, axis=-1)


def rwkv_recurrence(r, decay, k, v, kk, a, s0):
    tm = lambda t: jnp.moveaxis(t, 1, 0)

    def step(s, inp):
        r_t, d_t, k_t, v_t, kk_t, a_t = inp
        sa = jnp.einsum('bhij,bhj->bhi', s, kk_t)
        s = (s * d_t[:, :, None, :] - sa[..., None] * (kk_t * a_t)[:, :, None, :]
             + v_t[..., None] * k_t[:, :, None, :])
        return s, jnp.einsum('bhij,bhj->bhi', s, r_t)

    s_final, o = lax.scan(step, s0.astype(jnp.float32), (tm(r), tm(decay), tm(k), tm(v), tm(kk), tm(a)))
    return jnp.moveaxis(o, 0, 1), s_final


def rwkv7_branch(zs_mix, wkv_prev, w):
    f32 = jnp.float32
    bsz, t, _ = zs_mix.shape
    r, k, v, xw, xa, xg = jnp.split(zs_mix, A_SPLITS, axis=-1)
    w_log = -jax.nn.softplus(-(w['w0'] + jnp.tanh(xw) @ w['w2_decay']).astype(f32)) - 0.5
    decay = jnp.exp(-jnp.exp(w_log))
    a = jax.nn.sigmoid(w['a0'] + xa @ w['a2_iclr'])
    g = jax.nn.sigmoid(xg) @ w['g2_gate']
    heads = lambda u: u.reshape(bsz, t, A_HEADS, A_HEAD_DIM).astype(f32)
    kk = heads(k * w['k_k'])
    kk = kk / jnp.maximum(jnp.sqrt(jnp.sum(kk * kk, axis=-1, keepdims=True)), 1e-12)
    k = k * (1 + (a - 1) * w['k_a'])
    r_h, k_h, v_h, a_h, d_h = heads(r), heads(k), heads(v), heads(a), heads(decay)
    o, s_new = rwkv_recurrence(r_h, d_h, k_h, v_h, kk, a_h, wkv_prev)
    mu = jnp.mean(o, axis=-1, keepdims=True)
    var = jnp.mean(jnp.square(o - mu), axis=-1, keepdims=True)
    on = ((o - mu) * lax.rsqrt(var + GN_EPS)).reshape(bsz, t, A_WIDTH) * w['ln_x_w'] + w['ln_x_b']
    bonus = jnp.sum(r_h * k_h * w['r_k'], axis=-1, keepdims=True) * v_h
    out = (on + bonus.reshape(bsz, t, A_WIDTH)) * g
    return out.astype(zs_mix.dtype), s_new.astype(wkv_prev.dtype)


def fox_prompt(q, k, v, logf):
    bsz, t, nh, dh = q.shape
    nb = t // Q_BLOCK
    scale = dh ** -0.5
    F = jnp.cumsum(logf, axis=1)
    Fk = jnp.moveaxis(F, 2, 1)[:, :, None, :]
    kpos = jnp.arange(t)
    qb = q.reshape(bsz, nb, Q_BLOCK, nh, dh).swapaxes(0, 1)
    Fb = F.reshape(bsz, nb, Q_BLOCK, nh).swapaxes(0, 1)
    qpos = jnp.arange(t).reshape(nb, Q_BLOCK)

    def block(args):
        q_i, F_i, pos_i = args
        s = jnp.einsum('bqhd,bkhd->bhqk', q_i, k).astype(jnp.float32) * scale
        s = s + jnp.moveaxis(F_i, 2, 1)[..., None] - Fk
        s = jnp.where(kpos[None, None, None, :] <= pos_i[None, None, :, None], s, -jnp.inf)
        p = jax.nn.softmax(s, axis=-1)
        return jnp.einsum('bhqk,bkhd->bqhd', p.astype(v.dtype), v)

    o = lax.map(block, (qb, Fb, qpos))
    return o.swapaxes(0, 1).reshape(bsz, t, nh * dh)


def fox_sample(q, k, v, logf, past_k, past_v, past_logf):
    bsz, t, nh, dh = q.shape
    p_len = past_k.shape[1]
    scale = dh ** -0.5
    F = jnp.cumsum(jnp.concatenate([past_logf.astype(jnp.float32), logf], axis=1), axis=1)
    F_past = jnp.moveaxis(F[:, :p_len], 2, 1)[:, :, None, :]
    F_new = jnp.moveaxis(F[:, p_len:], 2, 1)
    s_past = (jnp.einsum('bqhd,bkhd->bhqk', q, past_k).astype(jnp.float32) * scale
              + F_new[..., None] - F_past)
    s_new = (jnp.einsum('bqhd,bkhd->bhqk', q, k).astype(jnp.float32) * scale
             + F_new[..., None] - F_new[:, :, None, :])
    s_new = jnp.where(jnp.tril(jnp.ones((t, t), bool)), s_new, -jnp.inf)
    p = jax.nn.softmax(jnp.concatenate([s_past, s_new], axis=-1), axis=-1)
    o = (jnp.einsum('bhqk,bkhd->bqhd', p[..., :p_len].astype(past_v.dtype), past_v)
         + jnp.einsum('bhqk,bkhd->bqhd', p[..., p_len:].astype(v.dtype), v))
    return o.reshape(bsz, t, nh * dh)


def peer_ffn(h, wq, keys1, keys2, u_tab, v_tab):
    f32 = jnp.float32
    n = h.shape[0]
    q = (h @ wq).reshape(n, PEER_HEADS, 2, PEER_HALF).astype(f32)
    s1 = jnp.einsum('thc,nc->thn', q[:, :, 0], keys1.astype(f32))
    s2 = jnp.einsum('thc,nc->thn', q[:, :, 1], keys2.astype(f32))
    t1, i1 = lax.top_k(s1, PEER_TOPK)
    t2, i2 = lax.top_k(s2, PEER_TOPK)
    cand_s = (t1[..., :, None] + t2[..., None, :]).reshape(n, PEER_HEADS, PEER_TOPK * PEER_TOPK)
    cand_i = (i1[..., :, None] * PEER_N_KEYS + i2[..., None, :]).reshape(n, PEER_HEADS, PEER_TOPK * PEER_TOPK)
    top_s, pos = lax.top_k(cand_s, PEER_TOPK)
    idx = jnp.take_along_axis(cand_i, pos, axis=-1)
    gate = jax.nn.softmax(top_s, axis=-1)
    u_e = u_tab[idx]
    v_e = v_tab[idx]
    act = jax.nn.gelu(jnp.einsum('td,thkd->thk', h, u_e).astype(f32))
    return jnp.einsum('thk,thkd->td', (gate * act).astype(h.dtype), v_e)


def peer_apply(h, wq, keys1, keys2, u_tab, v_tab):
    bsz, t, d = h.shape
    flat = h.reshape(bsz * t, d)
    n = flat.shape[0]
    ffn = lambda blk: peer_ffn(blk, wq, keys1, keys2, u_tab, v_tab)
    if n % PEER_BLOCK == 0 and n > PEER_BLOCK:
        out = lax.map(ffn, flat.reshape(n // PEER_BLOCK, PEER_BLOCK, d)).reshape(n, d)
    else:
        out = ffn(flat)
    return out.reshape(bsz, t, d)


def trunk_layer(x, c, shift_prev, wkv_prev, attend, w):
    bsz, t, _ = x.shape
    sh1, sc1, g1, sh2, sc2, g2 = ada_modulation(c, w['w_ada'], w['b_ada'])
    h = x * (1 + sc1) + sh1
    z = h @ w['w_in']
    zs = z[..., :SHIFT_COLS]
    zf = z[..., SHIFT_COLS:SHIFT_COLS + FOX_COLS]
    zg = z[..., SHIFT_COLS + FOX_COLS:]
    prev = jnp.concatenate([shift_prev[:, None, :].astype(zs.dtype), zs[:, :-1]], axis=1)
    zs_mix = zs + w['mu_shift'] * (prev - zs)
    a_out, wkv_new = rwkv7_branch(zs_mix, wkv_prev, w)
    q, k, v, f = jnp.split(zf, [B_WIDTH, 2 * B_WIDTH, 3 * B_WIDTH], axis=-1)
    heads = lambda u: u.reshape(bsz, t, B_HEADS, B_HEAD_DIM)
    q, k, v = heads(q), heads(k), heads(v)
    logf = jax.nn.log_sigmoid((f + w['b_forget']).astype(jnp.float32))
    b_out = attend(q, k, v, logf)
    gate_a, gate_b = jnp.split(zg, 2, axis=-1)
    mixed = (jax.nn.sigmoid(gate_a) * (a_out @ w['w_up_a'])
             + jax.nn.sigmoid(gate_b) * (b_out @ w['w_up_b']))
    x1 = layer_norm(DN_ALPHA * x + g1 * (mixed @ w['w_o']), w['ln1_g'], w['ln1_b'])
    h2 = x1 * (1 + sc2) + sh2
    ffn = peer_apply(h2, w['peer_wq'], w['peer_keys1'], w['peer_keys2'], w['peer_u'], w['peer_v'])
    y = layer_norm(DN_ALPHA * x1 + g2 * ffn, w['ln2_g'], w['ln2_b'])
    return y, k, v, logf, wkv_new, zs[:, -1]


def setup_inputs(seed: int = 0) -> dict:
    key = jax.random.key(seed)
    ks = iter(jax.random.split(key, 40))
    f32 = jnp.float32
    L = DEPTH
    n_pages = PAST_LEN // PAGE_SIZE
    n_pool = (DEC_BATCH * n_pages * 5) // 4

    def normal(shape, scale):
        return scale * jax.random.normal(next(ks), shape, f32)

    inp = {}
    inp['x_prompt'] = normal((BATCH, SEQ, D_MODEL), 1.0)
    inp['x_sample'] = normal((DEC_BATCH, DEC_SEQ, D_MODEL), 1.0)
    inp['cache_k'] = normal((L, n_pool, PAGE_SIZE, B_HEADS, B_HEAD_DIM), 1.0)
    inp['cache_v'] = normal((L, n_pool, PAGE_SIZE, B_HEADS, B_HEAD_DIM), 1.0)
    inp['cache_logf'] = jax.nn.log_sigmoid(3.0 + normal((L, n_pool, PAGE_SIZE, B_HEADS), 1.0))
    inp['state_wkv'] = normal((L, DEC_BATCH, A_HEADS, A_HEAD_DIM, A_HEAD_DIM), 0.3)
    inp['state_shift'] = normal((L, DEC_BATCH, SHIFT_COLS), 1.0)
    perm = jax.random.permutation(next(ks), n_pool)
    inp['page_table'] = perm[: DEC_BATCH * n_pages].reshape(DEC_BATCH, n_pages).astype(jnp.int32)
    inp['c_prompt'] = normal((BATCH, D_MODEL), 1.0)
    inp['c_sample'] = normal((DEC_BATCH, D_MODEL), 1.0)
    inp['w_ada'] = normal((L, D_MODEL, 6 * D_MODEL), D_MODEL ** -0.5)
    inp['b_ada'] = normal((L, 6 * D_MODEL), 0.02)
    inp['w_in'] = normal((L, D_MODEL, IN_COLS), D_MODEL ** -0.5)
    inp['mu_shift'] = jax.random.uniform(next(ks), (L, SHIFT_COLS), f32)
    inp['w0'] = jax.random.uniform(next(ks), (L, A_WIDTH), f32, -6.0, 1.0)
    inp['w2_decay'] = normal((L, DECAY_RANK, A_WIDTH), 0.5 * DECAY_RANK ** -0.5)
    inp['a0'] = normal((L, A_WIDTH), 0.5)
    inp['a2_iclr'] = normal((L, ICLR_RANK, A_WIDTH), ICLR_RANK ** -0.5)
    inp['g2_gate'] = normal((L, GATE_RANK, A_WIDTH), GATE_RANK ** -0.5)
    inp['k_k'] = 1.0 + normal((L, A_WIDTH), 0.1)
    inp['k_a'] = 1.0 + normal((L, A_WIDTH), 0.1)
    inp['r_k'] = normal((L, A_HEADS, A_HEAD_DIM), 0.1)
    inp['ln_x_w'] = 1.0 + normal((L, A_WIDTH), 0.1)
    inp['ln_x_b'] = normal((L, A_WIDTH), 0.02)
    inp['b_forget'] = jnp.linspace(1.0, 6.0, B_HEADS, dtype=f32) + normal((L, B_HEADS), 0.1)
    inp['w_up_a'] = normal((L, A_WIDTH, D_MODEL), A_WIDTH ** -0.5)
    inp['w_up_b'] = normal((L, B_WIDTH, D_MODEL), B_WIDTH ** -0.5)
    inp['w_o'] = normal((L, D_MODEL, D_MODEL), DN_BETA * D_MODEL ** -0.5)
    inp['ln1_g'] = 1.0 + normal((L, D_MODEL), 0.05)
    inp['ln1_b'] = normal((L, D_MODEL), 0.02)
    inp['peer_wq'] = normal((L, D_MODEL, PEER_HEADS * PEER_QDIM), D_MODEL ** -0.5)
    inp['peer_keys1'] = normal((L, PEER_N_KEYS, PEER_HALF), PEER_HALF ** -0.5)
    inp['peer_keys2'] = normal((L, PEER_N_KEYS, PEER_HALF), PEER_HALF ** -0.5)
    inp['peer_u'] = normal((L, PEER_N_EXPERTS, D_MODEL), D_MODEL ** -0.5)
    inp['peer_v'] = normal((L, PEER_N_EXPERTS, D_MODEL), DN_BETA)
    inp['ln2_g'] = 1.0 + normal((L, D_MODEL), 0.05)
    inp['ln2_b'] = normal((L, D_MODEL), 0.02)
    return inp


def reference(x_prompt, x_sample, cache_k, cache_v, cache_logf, state_wkv, state_shift, page_table,
              c_prompt, c_sample, w_ada, b_ada, w_in, mu_shift, w0, w2_decay, a0, a2_iclr, g2_gate,
              k_k, k_a, r_k, ln_x_w, ln_x_b, b_forget, w_up_a, w_up_b, w_o, ln1_g, ln1_b,
              peer_wq, peer_keys1, peer_keys2, peer_u, peer_v, ln2_g, ln2_b):
    n_seq, n_pages = page_table.shape
    yp, ys = x_prompt, x_sample
    outs_p = ([], [], [], [], [])
    outs_s = ([], [], [], [], [])
    for l in range(DEPTH):
        lw = {'w_ada': w_ada[l], 'b_ada': b_ada[l], 'w_in': w_in[l], 'mu_shift': mu_shift[l],
              'w0': w0[l], 'w2_decay': w2_decay[l], 'a0': a0[l], 'a2_iclr': a2_iclr[l],
              'g2_gate': g2_gate[l], 'k_k': k_k[l], 'k_a': k_a[l], 'r_k': r_k[l],
              'ln_x_w': ln_x_w[l], 'ln_x_b': ln_x_b[l], 'b_forget': b_forget[l],
              'w_up_a': w_up_a[l], 'w_up_b': w_up_b[l], 'w_o': w_o[l],
              'ln1_g': ln1_g[l], 'ln1_b': ln1_b[l], 'peer_wq': peer_wq[l],
              'peer_keys1': peer_keys1[l], 'peer_keys2': peer_keys2[l],
              'peer_u': peer_u[l], 'peer_v': peer_v[l], 'ln2_g': ln2_g[l], 'ln2_b': ln2_b[l]}
        shift0 = jnp.zeros((yp.shape[0], SHIFT_COLS), yp.dtype)
        wkv0 = jnp.zeros((yp.shape[0], A_HEADS, A_HEAD_DIM, A_HEAD_DIM), state_wkv.dtype)
        yp, kp, vp, lfp, wkvp, shp = trunk_layer(yp, c_prompt, shift0, wkv0, fox_prompt, lw)
        past_k = cache_k[l][page_table].reshape(n_seq, n_pages * PAGE_SIZE, B_HEADS, B_HEAD_DIM)
        past_v = cache_v[l][page_table].reshape(n_seq, n_pages * PAGE_SIZE, B_HEADS, B_HEAD_DIM)
        past_logf = cache_logf[l][page_table].reshape(n_seq, n_pages * PAGE_SIZE, B_HEADS)
        attend = functools.partial(fox_sample, past_k=past_k, past_v=past_v, past_logf=past_logf)
        ys, ks_, vs_, lfs, wkvs, shs = trunk_layer(ys, c_sample, state_shift[l], state_wkv[l], attend, lw)
        for lst, arr in zip(outs_p, (kp, vp, lfp, wkvp, shp)):
            lst.append(arr)
        for lst, arr in zip(outs_s, (ks_, vs_, lfs, wkvs, shs)):
            lst.append(arr)
    new_k_prompt = jnp.stack(outs_p[0]).astype(cache_k.dtype)
    new_v_prompt = jnp.stack(outs_p[1]).astype(cache_v.dtype)
    new_logf_prompt = jnp.stack(outs_p[2]).astype(cache_logf.dtype)
    new_wkv_prompt = jnp.stack(outs_p[3]).astype(state_wkv.dtype)
    new_shift_prompt = jnp.stack(outs_p[4]).astype(state_shift.dtype)
    new_k_sample = jnp.stack(outs_s[0]).astype(cache_k.dtype)
    new_v_sample = jnp.stack(outs_s[1]).astype(cache_v.dtype)
    new_logf_sample = jnp.stack(outs_s[2]).astype(cache_logf.dtype)
    new_wkv_sample = jnp.stack(outs_s[3]).astype(state_wkv.dtype)
    new_shift_sample = jnp.stack(outs_s[4]).astype(state_shift.dtype)
    return (yp, ys, new_k_prompt, new_v_prompt, new_logf_prompt, new_wkv_prompt, new_shift_prompt,
            new_k_sample, new_v_sample, new_logf_sample, new_wkv_sample, new_shift_sample)
```

```python
import functools
import math

import jax
import jax.numpy as jnp
from jax import lax
from jax.experimental import pallas as pl
from jax.experimental.pallas import tpu as pltpu

F32 = jnp.float32
BF16 = jnp.bfloat16

D_MODEL = 1024
HEADS = 8
HEAD_DIM = 64
WIDTH = HEADS * HEAD_DIM
DECAY_RANK = 64
ICLR_RANK = 64
GATE_RANK = 128
SHIFT_COLS = 3 * WIDTH + DECAY_RANK + ICLR_RANK + GATE_RANK
LANE = 128
FORGET_PAD = LANE
IN_COLS_PAD = SHIFT_COLS + 3 * WIDTH + FORGET_PAD + 2 * D_MODEL
GN_EPS = 64e-5
LN_EPS = 1e-5
DN_ALPHA = 2.0 ** 0.25
PAGE = 128
PEER_HEADS = 8
PEER_KEYS = 128
PEER_TOPK = 16
PEER_EXPERTS = PEER_KEYS * PEER_KEYS
VMEM_LIMIT = 56 * 1024 * 1024


def _params(sem):
    return pltpu.CompilerParams(dimension_semantics=sem, vmem_limit_bytes=VMEM_LIMIT)


def _layer_norm(x, g, b):
    mu = jnp.mean(x, axis=-1, keepdims=True)
    xc = x - mu
    var = jnp.mean(xc * xc, axis=-1, keepdims=True)
    return xc * lax.rsqrt(var + LN_EPS) * g + b


def _softplus(x):
    return jnp.maximum(x, 0.0) + jnp.log1p(jnp.exp(-jnp.abs(x)))


def _ada_kernel(c_ref, w_ref, b_ref, o_ref):
    c = c_ref[...]
    s = c * jax.nn.sigmoid(c)
    o_ref[...] = jnp.dot(s.astype(BF16), w_ref[...], preferred_element_type=F32) + b_ref[...]


def _ada(c, w_ada_bf, b_ada):
    rows = c.shape[0]
    n_out = w_ada_bf.shape[1]
    tn = 1024
    return pl.pallas_call(
        _ada_kernel,
        grid=(n_out // tn,),
        in_specs=[pl.BlockSpec((rows, D_MODEL), lambda j: (0, 0)),
                  pl.BlockSpec((D_MODEL, tn), lambda j: (0, j)),
                  pl.BlockSpec((1, tn), lambda j: (0, j))],
        out_specs=pl.BlockSpec((rows, tn), lambda j: (0, j)),
        out_shape=jax.ShapeDtypeStruct((rows, n_out), F32),
        compiler_params=_params(("parallel",)),
        name="ada",
    )(c, w_ada_bf, b_ada)


def _inproj_kernel(x_ref, sc_ref, sh_ref, prev_ref, w_ref, mu_ref, w0_ref, w2_ref, a0_ref, a2_ref,
                   g2_ref, kk_ref, ka_ref, bf_ref,
                   rw_ref, g_ref, q_ref, k_ref, v_ref, lf_ref, sg_ref, shift_ref, carry_ref,
                   *, tiles_per_seq, per_token):
    tm = x_ref.shape[0]
    h = x_ref[...] * (1.0 + sc_ref[...]) + sh_ref[...]
    hb = h.astype(BF16)
    c0 = SHIFT_COLS
    zs = jnp.dot(hb, w_ref[:, 0:c0], preferred_element_type=F32)
    if per_token:
        prev = prev_ref[...]
        shift_ref[...] = zs
    else:
        i = pl.program_id(0)
        first = jnp.where(i % tiles_per_seq == 0, prev_ref[...], carry_ref[...])
        rolled = pltpu.roll(zs, 1, axis=0)
        row = lax.broadcasted_iota(jnp.int32, zs.shape, 0)
        prev = jnp.where(row == 0, first, rolled)
        last = zs[tm - 1:tm, :]
        carry_ref[...] = last
        shift_ref[...] = last
    zm = zs + mu_ref[...] * (prev - zs)
    r = zm[:, 0:WIDTH]
    k = zm[:, WIDTH:2 * WIDTH]
    v = zm[:, 2 * WIDTH:3 * WIDTH]
    lowrank = zm[:, 3 * WIDTH:3 * WIDTH + LANE]
    xg = zm[:, 3 * WIDTH + LANE:SHIFT_COLS]
    wl = w0_ref[...] + jnp.dot(jnp.tanh(lowrank).astype(BF16), w2_ref[...], preferred_element_type=F32)
    w_log = -_softplus(-wl) - 0.5
    decay = jnp.exp(-jnp.exp(w_log))
    a = jax.nn.sigmoid(a0_ref[...] + jnp.dot(lowrank.astype(BF16), a2_ref[...], preferred_element_type=F32))
    g_ref[...] = jnp.dot(jax.nn.sigmoid(xg).astype(BF16), g2_ref[...], preferred_element_type=F32)
    rw_ref[:, 0 * WIDTH:1 * WIDTH] = r
    rw_ref[:, 1 * WIDTH:2 * WIDTH] = decay
    rw_ref[:, 2 * WIDTH:3 * WIDTH] = k * (1.0 + (a - 1.0) * ka_ref[...])
    rw_ref[:, 3 * WIDTH:4 * WIDTH] = v
    rw_ref[:, 4 * WIDTH:5 * WIDTH] = k * kk_ref[...]
    rw_ref[:, 5 * WIDTH:6 * WIDTH] = a
    q_ref[...] = (jnp.dot(hb, w_ref[:, c0:c0 + WIDTH], preferred_element_type=F32)
                  * (HEAD_DIM ** -0.5)).astype(q_ref.dtype)
    k_ref[...] = jnp.dot(hb, w_ref[:, c0 + WIDTH:c0 + 2 * WIDTH], preferred_element_type=F32)
    v_ref[...] = jnp.dot(hb, w_ref[:, c0 + 2 * WIDTH:c0 + 3 * WIDTH], preferred_element_type=F32)
    c1 = c0 + 3 * WIDTH
    f = jnp.dot(hb, w_ref[:, c1:c1 + FORGET_PAD], preferred_element_type=F32) + bf_ref[...]
    lf_ref[...] = -_softplus(-f)
    c2 = c1 + FORGET_PAD
    sg_ref[...] = jax.nn.sigmoid(jnp.dot(hb, w_ref[:, c2:c2 + 2 * D_MODEL], preferred_element_type=F32))


def _inproj(x2, sc1, sh1, prev, wts, *, seq_len, per_token, tm):
    n = x2.shape[0]
    tiles_per_seq = 1 if per_token else seq_len // tm
    n_seq = n // seq_len
    const = lambda shape: pl.BlockSpec(shape, lambda i: (0,) * len(shape))
    if per_token:
        mod_spec = pl.BlockSpec((tm, D_MODEL), lambda i: (i, 0))
        prev_spec = pl.BlockSpec((tm, SHIFT_COLS), lambda i: (i, 0))
        shift_spec = pl.BlockSpec((tm, SHIFT_COLS), lambda i: (i, 0))
        shift_shape = jax.ShapeDtypeStruct((n, SHIFT_COLS), F32)
    else:
        mod_spec = pl.BlockSpec((None, 1, D_MODEL), lambda i: (i // tiles_per_seq, 0, 0))
        prev_spec = pl.BlockSpec((None, 1, SHIFT_COLS), lambda i: (i // tiles_per_seq, 0, 0))
        shift_spec = pl.BlockSpec((None, 1, SHIFT_COLS), lambda i: (i // tiles_per_seq, 0, 0))
        shift_shape = jax.ShapeDtypeStruct((n_seq, 1, SHIFT_COLS), F32)
    row = lambda w: pl.BlockSpec((tm, w), lambda i: (i, 0))
    kern = functools.partial(_inproj_kernel, tiles_per_seq=tiles_per_seq, per_token=per_token)
    return pl.pallas_call(
        kern,
        grid=(n // tm,),
        in_specs=[row(D_MODEL), mod_spec, mod_spec, prev_spec,
                  pl.BlockSpec((D_MODEL, IN_COLS_PAD), lambda i: (0, 0), pipeline_mode=pl.Buffered(1)),
                  const((1, SHIFT_COLS)), const((1, WIDTH)), const((LANE, WIDTH)), const((1, WIDTH)),
                  const((LANE, WIDTH)), const((GATE_RANK, WIDTH)), const((1, WIDTH)), const((1, WIDTH)),
                  const((1, FORGET_PAD))],
        out_specs=[row(6 * WIDTH), row(WIDTH), row(WIDTH), row(WIDTH), row(WIDTH), row(FORGET_PAD),
                   row(2 * D_MODEL), shift_spec],
        out_shape=[jax.ShapeDtypeStruct((n, 6 * WIDTH), F32), jax.ShapeDtypeStruct((n, WIDTH), F32),
                   jax.ShapeDtypeStruct((n, WIDTH), BF16), jax.ShapeDtypeStruct((n, WIDTH), F32),
                   jax.ShapeDtypeStruct((n, WIDTH), F32), jax.ShapeDtypeStruct((n, FORGET_PAD), F32),
                   jax.ShapeDtypeStruct((n, 2 * D_MODEL), F32), shift_shape],
        scratch_shapes=[pltpu.VMEM((1, SHIFT_COLS), F32)],
        compiler_params=_params(("arbitrary",)),
        name="inproj",
    )(x2, sc1, sh1, prev, wts["w_in"], wts["mu"], wts["w0"], wts["w2"], wts["a0"], wts["a2"],
      wts["g2"], wts["k_k"], wts["k_a"], wts["b_forget"])


def _rec_kernel(rw_ref, s0_ref, rk_ref, lnw_ref, lnb_ref, o_ref, sout_ref, s_ref, obuf_ref):
    tb = rw_ref.shape[0]
    tj = pl.program_id(1)

    @pl.when(tj == 0)
    def _():
        s_ref[...] = s0_ref[...]

    def step(t, carry):
        r = rw_ref[t, 0]
        w = rw_ref[t, 1]
        k = rw_ref[t, 2]
        v = rw_ref[t, 3]
        kkr = rw_ref[t, 4]
        a = rw_ref[t, 5]
        n2 = jnp.sum(kkr * kkr, axis=0, keepdims=True)
        kk = kkr / jnp.maximum(jnp.sqrt(n2), 1e-12)
        b = kk * a
        for i in range(HEAD_DIM):
            si = s_ref[i]
            sa = jnp.sum(si * kk, axis=0, keepdims=True)
            sn = si * w - sa * b + rw_ref[t, 3, pl.ds(i, 1), :] * k
            s_ref[i] = sn
            obuf_ref[i:i + 1, :] = jnp.sum(sn * r, axis=0, keepdims=True)
        o = obuf_ref[...]
        mu = jnp.mean(o, axis=0, keepdims=True)
        oc = o - mu
        var = jnp.mean(oc * oc, axis=0, keepdims=True)
        on = oc * lax.rsqrt(var + GN_EPS) * lnw_ref[...] + lnb_ref[...]
        bonus = jnp.sum(r * k * rk_ref[...], axis=0, keepdims=True) * v
        o_ref[t] = on + bonus
        return carry

    lax.fori_loop(0, tb, step, 0)

    @pl.when(tj == pl.num_programs(1) - 1)
    def _():
        sout_ref[...] = s_ref[...]


def _rwkv_recurrence(rw_t, s0, rk_t, lnw_t, lnb_t, *, tb):
    t_len, _, _, lanes = rw_t.shape
    return pl.pallas_call(
        _rec_kernel,
        grid=(lanes // LANE, t_len // tb),
        in_specs=[pl.BlockSpec((tb, 6, HEAD_DIM, LANE), lambda l, t: (t, 0, 0, l)),
                  pl.BlockSpec((HEAD_DIM, HEAD_DIM, LANE), lambda l, t: (0, 0, l)),
                  pl.BlockSpec((HEAD_DIM, LANE), lambda l, t: (0, l)),
                  pl.BlockSpec((HEAD_DIM, LANE), lambda l, t: (0, l)),
                  pl.BlockSpec((HEAD_DIM, LANE), lambda l, t: (0, l))],
        out_specs=[pl.BlockSpec((tb, HEAD_DIM, LANE), lambda l, t: (t, 0, l)),
                   pl.BlockSpec((HEAD_DIM, HEAD_DIM, LANE), lambda l, t: (0, 0, l))],
        out_shape=[jax.ShapeDtypeStruct((t_len, HEAD_DIM, lanes), F32),
                   jax.ShapeDtypeStruct((HEAD_DIM, HEAD_DIM, lanes), F32)],
        scratch_shapes=[pltpu.VMEM((HEAD_DIM, HEAD_DIM, LANE), F32), pltpu.VMEM((HEAD_DIM, LANE), F32)],
        compiler_params=_params(("parallel", "arbitrary")),
        name="rwkv_recurrence",
    )(rw_t, s0, rk_t, lnw_t, lnb_t)


def _rwkv(rw, n_seq, seq_len, state_bhij, wts, *, tb):
    lanes = n_seq * HEADS
    rw_t = rw.reshape(n_seq, seq_len, 6, HEADS, HEAD_DIM).transpose(1, 2, 4, 0, 3).reshape(seq_len, 6, HEAD_DIM, lanes)
    s0 = state_bhij.transpose(2, 3, 0, 1).reshape(HEAD_DIM, HEAD_DIM, lanes)
    per_lane = lambda p: jnp.tile(p.reshape(HEADS, HEAD_DIM).T, (1, n_seq))
    o_t, s_new = _rwkv_recurrence(rw_t, s0, per_lane(wts["r_k"]), per_lane(wts["ln_x_w"]),
                                  per_lane(wts["ln_x_b"]), tb=tb)
    o = o_t.reshape(seq_len, HEAD_DIM, n_seq, HEADS).transpose(2, 0, 3, 1).reshape(n_seq * seq_len, WIDTH)
    s_out = s_new.reshape(HEAD_DIM, HEAD_DIM, n_seq, HEADS).transpose(2, 3, 0, 1)
    return o, s_out


def _cumsum_kernel(lf_ref, o_ref, carry_ref):
    tb = lf_ref.shape[0]

    @pl.when(pl.program_id(1) == 0)
    def _():
        carry_ref[...] = jnp.zeros_like(carry_ref)

    row = lax.broadcasted_iota(jnp.int32, (tb, tb), 0)
    col = lax.broadcasted_iota(jnp.int32, (tb, tb), 1)
    tri = (col <= row).astype(F32)
    f = jnp.dot(tri, lf_ref[...], precision=lax.Precision.HIGHEST, preferred_element_type=F32) + carry_ref[...]
    o_ref[...] = f
    carry_ref[...] = f[tb - 1:tb, :]


def _cumsum_time(lf3, *, tb=256):
    n_seq, t_len, w = lf3.shape
    return pl.pallas_call(
        _cumsum_kernel,
        grid=(n_seq, t_len // tb),
        in_specs=[pl.BlockSpec((None, tb, w), lambda b, t: (b, t, 0))],
        out_specs=pl.BlockSpec((None, tb, w), lambda b, t: (b, t, 0)),
        out_shape=jax.ShapeDtypeStruct(lf3.shape, F32),
        scratch_shapes=[pltpu.VMEM((1, w), F32)],
        compiler_params=_params(("parallel", "arbitrary")),
        name="logf_cumsum",
    )(lf3)


def _fox_prompt_kernel(q_ref, k_ref, v_ref, fr_ref, fc_ref, o_ref, *, tq):
    qi = pl.program_id(1)
    q = q_ref[...]
    fq = fc_ref[...]
    qpos = qi * tq + lax.broadcasted_iota(jnp.int32, (tq, tq), 0)
    kofs = lax.broadcasted_iota(jnp.int32, (tq, tq), 1)

    def body(ki, carry):
        m, l, acc = carry
        ks = pl.multiple_of(ki * tq, tq)
        k = k_ref[pl.ds(ks, tq), :]
        v = v_ref[pl.ds(ks, tq), :]
        fk = fr_ref[ki]
        s = lax.dot_general(q, k, (((1,), (1,)), ((), ())), preferred_element_type=F32)
        s = s + (fq - fk)
        s = jnp.where(ks + kofs <= qpos, s, -jnp.inf)
        m_new = jnp.maximum(m, jnp.max(s, axis=1, keepdims=True))
        p = jnp.exp(s - m_new)
        alpha = jnp.exp(m - m_new)
        l = alpha * l + jnp.sum(p, axis=1, keepdims=True)
        acc = alpha * acc + jnp.dot(p.astype(BF16), v, preferred_element_type=F32)
        return m_new, l, acc

    init = (jnp.full((tq, 1), -jnp.inf, F32), jnp.zeros((tq, 1), F32), jnp.zeros((tq, HEAD_DIM), F32))
    m, l, acc = lax.fori_loop(0, qi + 1, body, init)
    o_ref[...] = (acc / l).astype(o_ref.dtype)


def _fox_prompt(q, k, v, f_cum, n_seq, seq_len, *, tq):
    bh = n_seq * HEADS
    heads = lambda u: (u.astype(BF16).reshape(n_seq, seq_len, HEADS, HEAD_DIM)
                       .transpose(0, 2, 1, 3).reshape(bh, seq_len, HEAD_DIM))
    f_bh = f_cum[:, :, :HEADS].transpose(0, 2, 1).reshape(bh, seq_len)
    f_row = f_bh.reshape(bh, seq_len // tq, 1, tq)
    f_col = f_bh.reshape(bh, seq_len, 1)
    o = pl.pallas_call(
        functools.partial(_fox_prompt_kernel, tq=tq),
        grid=(bh, seq_len // tq),
        in_specs=[pl.BlockSpec((None, tq, HEAD_DIM), lambda b, i: (b, i, 0)),
                  pl.BlockSpec((None, seq_len, HEAD_DIM), lambda b, i: (b, 0, 0)),
                  pl.BlockSpec((None, seq_len, HEAD_DIM), lambda b, i: (b, 0, 0)),
                  pl.BlockSpec((None, seq_len // tq, 1, tq), lambda b, i: (b, 0, 0, 0)),
                  pl.BlockSpec((None, tq, 1), lambda b, i: (b, i, 0))],
        out_specs=pl.BlockSpec((None, tq, HEAD_DIM), lambda b, i: (b, i, 0)),
        out_shape=jax.ShapeDtypeStruct((bh, seq_len, HEAD_DIM), BF16),
        compiler_params=_params(("parallel", "arbitrary")),
        name="fox_prompt",
    )(heads(q), heads(k), heads(v), f_row, f_col)
    return o.reshape(n_seq, HEADS, seq_len, HEAD_DIM).transpose(0, 2, 1, 3).reshape(n_seq * seq_len, WIDTH)


def _fox_sample_kernel(pt_ref, qt_ref, knew_ref, vnew_ref, lfnew_ref, hm_ref, k_ref, v_ref, lft_ref,
                       o_ref, m_ref, l_ref, acc_ref, carry_ref):
    p = pl.program_id(1)
    qt = qt_ref[...]

    @pl.when(p == 0)
    def _():
        m_ref[...] = jnp.sum(qt * knew_ref[...], axis=1, keepdims=True)
        l_ref[...] = jnp.ones_like(l_ref)
        acc_ref[...] = jnp.broadcast_to(vnew_ref[...], acc_ref.shape)
        carry_ref[...] = lfnew_ref[...]

    kb = k_ref[...].astype(BF16)
    vb = v_ref[...].astype(BF16)
    s = lax.dot_general(qt.astype(BF16), kb, (((1,), (1,)), ((), ())), preferred_element_type=F32)
    lft = lft_ref[...]
    u = lax.broadcasted_iota(jnp.int32, (PAGE, PAGE), 0)
    t = lax.broadcasted_iota(jnp.int32, (PAGE, PAGE), 1)
    later = (u > t).astype(F32)
    suffix = jnp.dot(lft, later, precision=lax.Precision.HIGHEST, preferred_element_type=F32)
    carry = carry_ref[...]
    s = s + suffix + carry
    carry_ref[...] = carry + jnp.sum(lft, axis=1, keepdims=True)
    m_old = m_ref[...]
    m_new = jnp.maximum(m_old, jnp.max(s, axis=1, keepdims=True))
    pr = jnp.exp(s - m_new)
    alpha = jnp.exp(m_old - m_new)
    l_ref[...] = alpha * l_ref[...] + jnp.sum(pr, axis=1, keepdims=True)
    acc_ref[...] = alpha * acc_ref[...] + jnp.dot(pr.astype(BF16), vb, preferred_element_type=F32)
    m_ref[...] = m_new

    @pl.when(p == pl.num_programs(1) - 1)
    def _():
        o_ref[...] = jnp.sum(acc_ref[...] / l_ref[...] * hm_ref[...], axis=0, keepdims=True)


def _fox_sample(q, k_new, v_new, lf_new, cache_k, cache_v, cache_logf, page_table):
    n_seq, n_pages = page_table.shape
    n_pool = cache_k.shape[0]
    head_mask = (jnp.arange(WIDTH)[None, :] // HEAD_DIM == jnp.arange(HEADS)[:, None]).astype(F32)
    qt = q[:, None, :] * head_mask[None]
    lfn = jnp.broadcast_to(lf_new[:, :HEADS, None], (n_seq, HEADS, LANE))
    ck = cache_k.reshape(n_pool, PAGE, WIDTH)
    cv = cache_v.reshape(n_pool, PAGE, WIDTH)
    clt = cache_logf.transpose(0, 2, 1)
    page = lambda b, p, pt: (pt[b, n_pages - 1 - p], 0, 0)
    seq3 = lambda b, p, pt: (b, 0, 0)
    grid_spec = pltpu.PrefetchScalarGridSpec(
        num_scalar_prefetch=1,
        grid=(n_seq, n_pages),
        in_specs=[pl.BlockSpec((None, HEADS, WIDTH), seq3),
                  pl.BlockSpec((None, 1, WIDTH), seq3),
                  pl.BlockSpec((None, 1, WIDTH), seq3),
                  pl.BlockSpec((None, HEADS, LANE), seq3),
                  pl.BlockSpec((HEADS, WIDTH), lambda b, p, pt: (0, 0)),
                  pl.BlockSpec((None, PAGE, WIDTH), page),
                  pl.BlockSpec((None, PAGE, WIDTH), page),
                  pl.BlockSpec((None, HEADS, PAGE), page)],
        out_specs=pl.BlockSpec((None, 1, WIDTH), seq3),
        scratch_shapes=[pltpu.VMEM((HEADS, 1), F32), pltpu.VMEM((HEADS, 1), F32),
                        pltpu.VMEM((HEADS, WIDTH), F32), pltpu.VMEM((HEADS, LANE), F32)],
    )
    o = pl.pallas_call(
        _fox_sample_kernel,
        grid_spec=grid_spec,
        out_shape=jax.ShapeDtypeStruct((n_seq, 1, WIDTH), F32),
        compiler_params=_params(("parallel", "arbitrary")),
        name="fox_sample",
    )(page_table, qt, k_new[:, None, :], v_new[:, None, :], lfn, head_mask, ck, cv, clt)
    return o.reshape(n_seq, WIDTH)


def _mix_kernel(on_ref, g_ref, bo_ref, sg_ref, x_ref, g1_ref, sc2_ref, sh2_ref, wa_ref, wb_ref, wo_ref,
                l1g_ref, l1b_ref, wq_ref, x1_ref, h2_ref, qp_ref):
    a_out = (on_ref[...] * g_ref[...]).astype(BF16)
    ua = jnp.dot(a_out, wa_ref[...], preferred_element_type=F32)
    ub = jnp.dot(bo_ref[...].astype(BF16), wb_ref[...], preferred_element_type=F32)
    mixed = sg_ref[:, 0:D_MODEL] * ua + sg_ref[:, D_MODEL:2 * D_MODEL] * ub
    mo = jnp.dot(mixed.astype(BF16), wo_ref[...], preferred_element_type=F32)
    x1 = _layer_norm(DN_ALPHA * x_ref[...] + g1_ref[...] * mo, l1g_ref[...], l1b_ref[...])
    x1_ref[...] = x1
    h2 = (x1 * (1.0 + sc2_ref[...]) + sh2_ref[...]).astype(BF16)
    h2_ref[...] = h2
    qp_ref[...] = jnp.dot(h2, wq_ref[...], preferred_element_type=F32).astype(qp_ref.dtype)


def _mix(on, g, bo, sg, x2, g1, sc2, sh2, wts, *, seq_len, per_token, tm):
    n = x2.shape[0]
    tiles_per_seq = 1 if per_token else seq_len // tm
    if per_token:
        mod_spec = pl.BlockSpec((tm, D_MODEL), lambda i: (i, 0))
    else:
        mod_spec = pl.BlockSpec((None, 1, D_MODEL), lambda i: (i // tiles_per_seq, 0, 0))
    row = lambda w: pl.BlockSpec((tm, w), lambda i: (i, 0))
    res = lambda shape: pl.BlockSpec(shape, lambda i: (0, 0), pipeline_mode=pl.Buffered(1))
    nq = PEER_HEADS * 2 * PEER_KEYS
    return pl.pallas_call(
        _mix_kernel,
        grid=(n // tm,),
        in_specs=[row(WIDTH), row(WIDTH), row(WIDTH), row(2 * D_MODEL), row(D_MODEL),
                  mod_spec, mod_spec, mod_spec,
                  res((WIDTH, D_MODEL)), res((WIDTH, D_MODEL)), res((D_MODEL, D_MODEL)),
                  res((1, D_MODEL)), res((1, D_MODEL)), res((D_MODEL, nq))],
        out_specs=[row(D_MODEL), row(D_MODEL), row(nq)],
        out_shape=[jax.ShapeDtypeStruct((n, D_MODEL), F32), jax.ShapeDtypeStruct((n, D_MODEL), BF16),
                   jax.ShapeDtypeStruct((n, nq), BF16)],
        compiler_params=_params(("parallel",)),
        name="mix_ln1_peerq",
    )(on, g, bo, sg, x2, g1, sc2, sh2, wts["w_up_a"], wts["w_up_b"], wts["w_o"], wts["ln1_g"], wts["ln1_b"],
      wts["peer_wq"])


def _top_rows(s, payload, count):
    rows = s.shape[0]
    rid = lax.broadcasted_iota(jnp.int32, s.shape, 0).astype(F32)
    vals, pays = [], []
    for _ in range(count):
        m = jnp.max(s, axis=0, keepdims=True)
        am = jnp.min(jnp.where(s == m, rid, float(rows)), axis=0, keepdims=True)
        hit = rid == am
        vals.append(m)
        pays.append(jnp.max(jnp.where(hit, payload, -1.0), axis=0, keepdims=True))
        s = jnp.where(hit, -jnp.inf, s)
    return jnp.concatenate(vals, axis=0), jnp.concatenate(pays, axis=0)


def _peer_topk_kernel(qp_ref, k1_ref, k2_ref, idx_ref, gate_ref):
    tm = qp_ref.shape[0]
    key_id = lax.broadcasted_iota(jnp.int32, (PEER_KEYS, tm), 0).astype(F32)
    nt = (((1,), (1,)), ((), ()))
    for h in range(PEER_HEADS):
        q1 = qp_ref[:, (2 * h) * PEER_KEYS:(2 * h + 1) * PEER_KEYS]
        q2 = qp_ref[:, (2 * h + 1) * PEER_KEYS:(2 * h + 2) * PEER_KEYS]
        s1 = lax.dot_general(k1_ref[...], q1, nt, preferred_element_type=F32)
        s2 = lax.dot_general(k2_ref[...], q2, nt, preferred_element_type=F32)
        t1, i1 = _top_rows(s1, key_id, PEER_TOPK)
        t2, i2 = _top_rows(s2, key_id, PEER_TOPK)
        cand_s = jnp.concatenate([t1[a:a + 1, :] + t2 for a in range(PEER_TOPK)], axis=0)
        cand_i = jnp.concatenate([i1[a:a + 1, :] * PEER_KEYS + i2 for a in range(PEER_TOPK)], axis=0)
        top_s, top_i = _top_rows(cand_s, cand_i, PEER_TOPK)
        e = jnp.exp(top_s - top_s[0:1, :])
        gate = e / jnp.sum(e, axis=0, keepdims=True)
        idx_ref[h * PEER_TOPK:(h + 1) * PEER_TOPK, :] = top_i.astype(jnp.int32)
        gate_ref[h * PEER_TOPK:(h + 1) * PEER_TOPK, :] = gate


def _peer_topk(qp, keys1_bf, keys2_bf, *, tm):
    n = qp.shape[0]
    nq = qp.shape[1]
    slots = PEER_HEADS * PEER_TOPK
    return pl.pallas_call(
        _peer_topk_kernel,
        grid=(n // tm,),
        in_specs=[pl.BlockSpec((tm, nq), lambda i: (i, 0)),
                  pl.BlockSpec((PEER_KEYS, PEER_KEYS), lambda i: (0, 0)),
                  pl.BlockSpec((PEER_KEYS, PEER_KEYS), lambda i: (0, 0))],
        out_specs=[pl.BlockSpec((slots, tm), lambda i: (0, i)), pl.BlockSpec((slots, tm), lambda i: (0, i))],
        out_shape=[jax.ShapeDtypeStruct((slots, n), jnp.int32), jax.ShapeDtypeStruct((slots, n), F32)],
        compiler_params=_params(("parallel",)),
        name="peer_topk",
    )(qp, keys1_bf, keys2_bf)


def _peer_dense_kernel(h2_ref, gt_ref, u_ref, v_ref, x1_ref, g2_ref, l2g_ref, l2b_ref, y_ref, acc_ref):
    j = pl.program_id(1)

    @pl.when(j == 0)
    def _():
        acc_ref[...] = jnp.zeros_like(acc_ref)

    act = lax.dot_general(h2_ref[...], u_ref[...], (((1,), (1,)), ((), ())), preferred_element_type=F32)
    pw = (gt_ref[...] * jax.nn.gelu(act)).astype(BF16)
    acc_ref[...] += jnp.dot(pw, v_ref[...], preferred_element_type=F32)

    @pl.when(j == pl.num_programs(1) - 1)
    def _():
        y_ref[...] = _layer_norm(DN_ALPHA * x1_ref[...] + g2_ref[...] * acc_ref[...], l2g_ref[...], l2b_ref[...])


def _peer_dense(h2, gates, u_bf, v_bf, x1, g2, wts, *, seq_len, per_token, tm, te):
    n = h2.shape[0]
    tiles_per_seq = 1 if per_token else seq_len // tm
    if per_token:
        mod_spec = pl.BlockSpec((tm, D_MODEL), lambda i, j: (i, 0))
    else:
        mod_spec = pl.BlockSpec((None, 1, D_MODEL), lambda i, j: (i // tiles_per_seq, 0, 0))
    return pl.pallas_call(
        _peer_dense_kernel,
        grid=(n // tm, PEER_EXPERTS // te),
        in_specs=[pl.BlockSpec((tm, D_MODEL), lambda i, j: (i, 0)),
                  pl.BlockSpec((tm, te), lambda i, j: (i, j)),
                  pl.BlockSpec((te, D_MODEL), lambda i, j: (j, 0)),
                  pl.BlockSpec((te, D_MODEL), lambda i, j: (j, 0)),
                  pl.BlockSpec((tm, D_MODEL), lambda i, j: (i, 0)),
                  mod_spec,
                  pl.BlockSpec((1, D_MODEL), lambda i, j: (0, 0)),
                  pl.BlockSpec((1, D_MODEL), lambda i, j: (0, 0))],
        out_specs=pl.BlockSpec((tm, D_MODEL), lambda i, j: (i, 0)),
        out_shape=jax.ShapeDtypeStruct((n, D_MODEL), F32),
        scratch_shapes=[pltpu.VMEM((tm, D_MODEL), F32)],
        compiler_params=_params(("parallel", "arbitrary")),
        name="peer_dense_ln2",
    )(h2, gates, u_bf, v_bf, x1, g2, wts["ln2_g"], wts["ln2_b"])


def _scatter_gates(idx_t, gate_t):
    n = idx_t.shape[1]
    rows = jnp.broadcast_to(jnp.arange(n, dtype=jnp.int32)[None, :], idx_t.shape)
    return jnp.zeros((n, PEER_EXPERTS), F32).at[rows.reshape(-1), idx_t.reshape(-1)].add(gate_t.reshape(-1))


def _layer(x, mods, shift_prev, wkv_prev, attend, wts, *, per_token):
    n_seq, seq_len, _ = x.shape
    n = n_seq * seq_len
    x2 = x.reshape(n, D_MODEL)
    if per_token:
        tm_in = tm_mix = tm_topk = tm_peer = n
        te = 512
        sh1, sc1, g1, sh2, sc2, g2 = mods
        prev = shift_prev
    else:
        tm_in, tm_mix, tm_topk, tm_peer, te = 256, 256, 256, min(1024, seq_len), 512
        sh1, sc1, g1, sh2, sc2, g2 = (m[:, None, :] for m in mods)
        prev = shift_prev[:, None, :]
    rw, g, q, k, v, lf, sg, shift = _inproj(x2, sc1, sh1, prev, wts, seq_len=seq_len, per_token=per_token, tm=tm_in)
    on, wkv_new = _rwkv(rw, n_seq, seq_len, wkv_prev, wts, tb=min(32, seq_len))
    bo = attend(q, k, v, lf)
    x1, h2, qp = _mix(on, g, bo, sg, x2, g1, sc2, sh2, wts, seq_len=seq_len, per_token=per_token, tm=tm_mix)
    idx_t, gate_t = _peer_topk(qp, wts["peer_keys1"], wts["peer_keys2"], tm=tm_topk)
    gates = _scatter_gates(idx_t, gate_t)
    y = _peer_dense(h2, gates, wts["peer_u"], wts["peer_v"], x1, g2, wts, seq_len=seq_len,
                    per_token=per_token, tm=tm_peer, te=te)
    return (y.reshape(n_seq, seq_len, D_MODEL), k.reshape(n_seq, seq_len, HEADS, HEAD_DIM),
            v.reshape(n_seq, seq_len, HEADS, HEAD_DIM), lf[:, :HEADS].reshape(n_seq, seq_len, HEADS),
            wkv_new, shift.reshape(n_seq, SHIFT_COLS))


def _layer_weights(l, w_in, mu_shift, w0, w2_decay, a0, a2_iclr, g2_gate, k_k, k_a, r_k, ln_x_w, ln_x_b,
                   b_forget, w_up_a, w_up_b, w_o, ln1_g, ln1_b, peer_wq, peer_keys1, peer_keys2, peer_u,
                   peer_v, ln2_g, ln2_b):
    wi = w_in[l]
    c = SHIFT_COLS + 3 * WIDTH
    w_in_pad = jnp.concatenate(
        [wi[:, :c], jnp.pad(wi[:, c:c + HEADS], ((0, 0), (0, FORGET_PAD - HEADS))), wi[:, c + HEADS:]], axis=1)
    row = lambda p: p[l].reshape(1, -1)
    zeros = jnp.zeros((LANE - DECAY_RANK, WIDTH), F32)
    return {
        "w_in": w_in_pad.astype(BF16), "mu": row(mu_shift), "w0": row(w0),
        "w2": jnp.concatenate([w2_decay[l], zeros], axis=0).astype(BF16),
        "a0": row(a0),
        "a2": jnp.concatenate([zeros, a2_iclr[l]], axis=0).astype(BF16),
        "g2": g2_gate[l].astype(BF16), "k_k": row(k_k), "k_a": row(k_a),
        "b_forget": jnp.pad(b_forget[l], (0, FORGET_PAD - HEADS)).reshape(1, FORGET_PAD),
        "r_k": r_k[l].reshape(-1), "ln_x_w": ln_x_w[l], "ln_x_b": ln_x_b[l],
        "w_up_a": w_up_a[l].astype(BF16), "w_up_b": w_up_b[l].astype(BF16), "w_o": w_o[l].astype(BF16),
        "ln1_g": row(ln1_g), "ln1_b": row(ln1_b), "peer_wq": peer_wq[l].astype(BF16),
        "peer_keys1": peer_keys1[l].astype(BF16), "peer_keys2": peer_keys2[l].astype(BF16),
        "peer_u": peer_u[l].astype(BF16), "peer_v": peer_v[l].astype(BF16),
        "ln2_g": row(ln2_g), "ln2_b": row(ln2_b),
    }


def kernel(x_prompt, x_sample, cache_k, cache_v, cache_logf, state_wkv, state_shift, page_table, c_prompt, c_sample, w_ada, b_ada, w_in, mu_shift, w0, w2_decay, a0, a2_iclr, g2_gate, k_k, k_a, r_k, ln_x_w, ln_x_b, b_forget, w_up_a, w_up_b, w_o, ln1_g, ln1_b, peer_wq, peer_keys1, peer_keys2, peer_u, peer_v, ln2_g, ln2_b):
    depth = w_in.shape[0]
    n_p, t_p, _ = x_prompt.shape
    n_s, t_s, _ = x_sample.shape
    assert t_s == 1, "the sample group is served one new token per sequence"
    yp, ys = x_prompt, x_sample
    outs_p, outs_s = [], []
    c_all = jnp.concatenate([c_prompt, c_sample], axis=0)
    for l in range(depth):
        wts = _layer_weights(l, w_in, mu_shift, w0, w2_decay, a0, a2_iclr, g2_gate, k_k, k_a, r_k, ln_x_w,
                             ln_x_b, b_forget, w_up_a, w_up_b, w_o, ln1_g, ln1_b, peer_wq, peer_keys1,
                             peer_keys2, peer_u, peer_v, ln2_g, ln2_b)
        mod = _ada(c_all, w_ada[l].astype(BF16), b_ada[l].reshape(1, -1))
        mods_p = tuple(mod[:n_p, i * D_MODEL:(i + 1) * D_MODEL] for i in range(6))
        mods_s = tuple(mod[n_p:, i * D_MODEL:(i + 1) * D_MODEL] for i in range(6))

        def attend_prompt(q, k, v, lf):
            f_cum = _cumsum_time(lf.reshape(n_p, t_p, FORGET_PAD))
            return _fox_prompt(q, k, v, f_cum, n_p, t_p, tq=256)

        def attend_sample(q, k, v, lf, l=l):
            return _fox_sample(q.astype(F32), k, v, lf, cache_k[l], cache_v[l], cache_logf[l], page_table)

        shift0 = jnp.zeros((n_p, SHIFT_COLS), F32)
        wkv0 = jnp.zeros((n_p, HEADS, HEAD_DIM, HEAD_DIM), F32)
        res_p = _layer(yp, mods_p, shift0, wkv0, attend_prompt, wts, per_token=False)
        res_s = _layer(ys, mods_s, state_shift[l], state_wkv[l], attend_sample, wts, per_token=True)
        yp, ys = res_p[0], res_s[0]
        outs_p.append(res_p[1:])
        outs_s.append(res_s[1:])
    stack = lambda outs, i, dt: jnp.stack([o[i] for o in outs]).astype(dt)
    return (yp, ys,
            stack(outs_p, 0, cache_k.dtype), stack(outs_p, 1, cache_v.dtype), stack(outs_p, 2, cache_logf.dtype),
            stack(outs_p, 3, state_wkv.dtype), stack(outs_p, 4, state_shift.dtype),
            stack(outs_s, 0, cache_k.dtype), stack(outs_s, 1, cache_v.dtype), stack(outs_s, 2, cache_logf.dtype),
            stack(outs_s, 3, state_wkv.dtype), stack(outs_s, 4, state_shift.dtype))
```

```python
import functools
import math

import jax
import jax.numpy as jnp
from jax import lax
from jax.experimental import pallas as pl
from jax.experimental.pallas import tpu as pltpu
from jax.experimental.pallas import tpu_sc as plsc

F32 = jnp.float32
BF16 = jnp.bfloat16

D_MODEL = 1024
HEADS = 8
HEAD_DIM = 64
WIDTH = HEADS * HEAD_DIM
DECAY_RANK = 64
ICLR_RANK = 64
GATE_RANK = 128
SHIFT_COLS = 3 * WIDTH + DECAY_RANK + ICLR_RANK + GATE_RANK
LANE = 128
FORGET_PAD = LANE
IN_COLS_PAD = SHIFT_COLS + 3 * WIDTH + FORGET_PAD + 2 * D_MODEL
GN_EPS = 64e-5
LN_EPS = 1e-5
DN_ALPHA = 2.0 ** 0.25
PAGE = 128
PEER_HEADS = 8
PEER_KEYS = 128
PEER_TOPK = 16
PEER_EXPERTS = PEER_KEYS * PEER_KEYS
PEER_SLOTS = PEER_HEADS * PEER_TOPK
VMEM_LIMIT = 56 * 1024 * 1024


def _params(sem):
    return pltpu.CompilerParams(dimension_semantics=sem, vmem_limit_bytes=VMEM_LIMIT)


def _layer_norm(x, g, b):
    mu = jnp.mean(x, axis=-1, keepdims=True)
    xc = x - mu
    var = jnp.mean(xc * xc, axis=-1, keepdims=True)
    return xc * lax.rsqrt(var + LN_EPS) * g + b


def _softplus(x):
    return jnp.maximum(x, 0.0) + jnp.log1p(jnp.exp(-jnp.abs(x)))


def _ada_kernel(c_ref, w_ref, b_ref, o_ref):
    c = c_ref[...]
    s = c * jax.nn.sigmoid(c)
    o_ref[...] = jnp.dot(s.astype(BF16), w_ref[...], preferred_element_type=F32) + b_ref[...]


def _ada(c, w_ada_bf, b_ada):
    rows = c.shape[0]
    n_out = w_ada_bf.shape[1]
    tn = 1024
    return pl.pallas_call(
        _ada_kernel,
        grid=(n_out // tn,),
        in_specs=[pl.BlockSpec((rows, D_MODEL), lambda j: (0, 0)),
                  pl.BlockSpec((D_MODEL, tn), lambda j: (0, j)),
                  pl.BlockSpec((1, tn), lambda j: (0, j))],
        out_specs=pl.BlockSpec((rows, tn), lambda j: (0, j)),
        out_shape=jax.ShapeDtypeStruct((rows, n_out), F32),
        compiler_params=_params(("parallel",)),
        name="ada",
    )(c, w_ada_bf, b_ada)


def _inproj_kernel(x_ref, sc_ref, sh_ref, prev_ref, w_ref, mu_ref, w0_ref, w2_ref, a0_ref, a2_ref,
                   g2_ref, kk_ref, ka_ref, bf_ref,
                   rw_ref, g_ref, q_ref, k_ref, v_ref, kb_ref, vb_ref, lf_ref, sg_ref, shift_ref, carry_ref,
                   *, tiles_per_seq, per_token):
    tm = x_ref.shape[0]
    h = x_ref[...] * (1.0 + sc_ref[...]) + sh_ref[...]
    hb = h.astype(BF16)
    c0 = SHIFT_COLS
    zs = jnp.dot(hb, w_ref[:, 0:c0], preferred_element_type=F32)
    if per_token:
        prev = prev_ref[...]
        shift_ref[...] = zs
    else:
        i = pl.program_id(0)
        first = jnp.where(i % tiles_per_seq == 0, prev_ref[...], carry_ref[...])
        rolled = pltpu.roll(zs, 1, axis=0)
        row = lax.broadcasted_iota(jnp.int32, zs.shape, 0)
        prev = jnp.where(row == 0, first, rolled)
        last = zs[tm - 1:tm, :]
        carry_ref[...] = last
        shift_ref[...] = last
    zm = zs + mu_ref[...] * (prev - zs)
    r = zm[:, 0:WIDTH]
    k = zm[:, WIDTH:2 * WIDTH]
    v = zm[:, 2 * WIDTH:3 * WIDTH]
    lowrank = zm[:, 3 * WIDTH:3 * WIDTH + LANE]
    xg = zm[:, 3 * WIDTH + LANE:SHIFT_COLS]
    wl = w0_ref[...] + jnp.dot(jnp.tanh(lowrank).astype(BF16), w2_ref[...], preferred_element_type=F32)
    w_log = -_softplus(-wl) - 0.5
    decay = jnp.exp(-jnp.exp(w_log))
    a = jax.nn.sigmoid(a0_ref[...] + jnp.dot(lowrank.astype(BF16), a2_ref[...], preferred_element_type=F32))
    g_ref[...] = jnp.dot(jax.nn.sigmoid(xg).astype(BF16), g2_ref[...], preferred_element_type=F32)
    rw_ref[:, 0 * WIDTH:1 * WIDTH] = r
    rw_ref[:, 1 * WIDTH:2 * WIDTH] = decay
    rw_ref[:, 2 * WIDTH:3 * WIDTH] = k * (1.0 + (a - 1.0) * ka_ref[...])
    rw_ref[:, 3 * WIDTH:4 * WIDTH] = v
    rw_ref[:, 4 * WIDTH:5 * WIDTH] = k * kk_ref[...]
    rw_ref[:, 5 * WIDTH:6 * WIDTH] = a
    q_ref[...] = (jnp.dot(hb, w_ref[:, c0:c0 + WIDTH], preferred_element_type=F32)
                  * (HEAD_DIM ** -0.5)).astype(q_ref.dtype)
    kf = jnp.dot(hb, w_ref[:, c0 + WIDTH:c0 + 2 * WIDTH], preferred_element_type=F32)
    vf = jnp.dot(hb, w_ref[:, c0 + 2 * WIDTH:c0 + 3 * WIDTH], preferred_element_type=F32)
    k_ref[...] = kf
    v_ref[...] = vf
    kb_ref[...] = kf.astype(BF16)
    vb_ref[...] = vf.astype(BF16)
    c1 = c0 + 3 * WIDTH
    f = jnp.dot(hb, w_ref[:, c1:c1 + FORGET_PAD], preferred_element_type=F32) + bf_ref[...]
    lf_ref[...] = -_softplus(-f)
    c2 = c1 + FORGET_PAD
    sg_ref[...] = jax.nn.sigmoid(jnp.dot(hb, w_ref[:, c2:c2 + 2 * D_MODEL], preferred_element_type=F32))


def _inproj(x2, sc1, sh1, prev, wts, *, seq_len, per_token, tm):
    n = x2.shape[0]
    tiles_per_seq = 1 if per_token else seq_len // tm
    n_seq = n // seq_len
    const = lambda shape: pl.BlockSpec(shape, lambda i: (0,) * len(shape))
    if per_token:
        mod_spec = pl.BlockSpec((tm, D_MODEL), lambda i: (i, 0))
        prev_spec = pl.BlockSpec((tm, SHIFT_COLS), lambda i: (i, 0))
        shift_spec = pl.BlockSpec((tm, SHIFT_COLS), lambda i: (i, 0))
        shift_shape = jax.ShapeDtypeStruct((n, SHIFT_COLS), F32)
    else:
        mod_spec = pl.BlockSpec((None, 1, D_MODEL), lambda i: (i // tiles_per_seq, 0, 0))
        prev_spec = pl.BlockSpec((None, 1, SHIFT_COLS), lambda i: (i // tiles_per_seq, 0, 0))
        shift_spec = pl.BlockSpec((None, 1, SHIFT_COLS), lambda i: (i // tiles_per_seq, 0, 0))
        shift_shape = jax.ShapeDtypeStruct((n_seq, 1, SHIFT_COLS), F32)
    row = lambda w: pl.BlockSpec((tm, w), lambda i: (i, 0))
    kern = functools.partial(_inproj_kernel, tiles_per_seq=tiles_per_seq, per_token=per_token)
    return pl.pallas_call(
        kern,
        grid=(n // tm,),
        in_specs=[row(D_MODEL), mod_spec, mod_spec, prev_spec,
                  pl.BlockSpec((D_MODEL, IN_COLS_PAD), lambda i: (0, 0), pipeline_mode=pl.Buffered(1)),
                  const((1, SHIFT_COLS)), const((1, WIDTH)), const((LANE, WIDTH)), const((1, WIDTH)),
                  const((LANE, WIDTH)), const((GATE_RANK, WIDTH)), const((1, WIDTH)), const((1, WIDTH)),
                  const((1, FORGET_PAD))],
        out_specs=[row(6 * WIDTH), row(WIDTH), row(WIDTH), row(WIDTH), row(WIDTH), row(WIDTH), row(WIDTH),
                   row(FORGET_PAD), row(2 * D_MODEL), shift_spec],
        out_shape=[jax.ShapeDtypeStruct((n, 6 * WIDTH), F32), jax.ShapeDtypeStruct((n, WIDTH), F32),
                   jax.ShapeDtypeStruct((n, WIDTH), BF16), jax.ShapeDtypeStruct((n, WIDTH), F32),
                   jax.ShapeDtypeStruct((n, WIDTH), F32), jax.ShapeDtypeStruct((n, WIDTH), BF16),
                   jax.ShapeDtypeStruct((n, WIDTH), BF16), jax.ShapeDtypeStruct((n, FORGET_PAD), F32),
                   jax.ShapeDtypeStruct((n, 2 * D_MODEL), F32), shift_shape],
        scratch_shapes=[pltpu.VMEM((1, SHIFT_COLS), F32)],
        compiler_params=_params(("arbitrary",)),
        name="inproj",
    )(x2, sc1, sh1, prev, wts["w_in"], wts["mu"], wts["w0"], wts["w2"], wts["a0"], wts["a2"],
      wts["g2"], wts["k_k"], wts["k_a"], wts["b_forget"])


def _rec_kernel(rw_ref, s0_ref, rk_ref, lnw_ref, lnb_ref, o_ref, sout_ref, s_ref, obuf_ref):
    tb = rw_ref.shape[0]
    tj = pl.program_id(1)

    @pl.when(tj == 0)
    def _():
        s_ref[...] = s0_ref[...]

    def step(t, carry):
        r = rw_ref[t, 0]
        w = rw_ref[t, 1]
        k = rw_ref[t, 2]
        v = rw_ref[t, 3]
        kkr = rw_ref[t, 4]
        a = rw_ref[t, 5]
        n2 = jnp.sum(kkr * kkr, axis=0, keepdims=True)
        kk = kkr / jnp.maximum(jnp.sqrt(n2), 1e-12)
        b = kk * a
        for i in range(HEAD_DIM):
            si = s_ref[i]
            sa = jnp.sum(si * kk, axis=0, keepdims=True)
            sn = si * w - sa * b + rw_ref[t, 3, pl.ds(i, 1), :] * k
            s_ref[i] = sn
            obuf_ref[i:i + 1, :] = jnp.sum(sn * r, axis=0, keepdims=True)
        o = obuf_ref[...]
        mu = jnp.mean(o, axis=0, keepdims=True)
        oc = o - mu
        var = jnp.mean(oc * oc, axis=0, keepdims=True)
        on = oc * lax.rsqrt(var + GN_EPS) * lnw_ref[...] + lnb_ref[...]
        bonus = jnp.sum(r * k * rk_ref[...], axis=0, keepdims=True) * v
        o_ref[t] = on + bonus
        return carry

    lax.fori_loop(0, tb, step, 0)

    @pl.when(tj == pl.num_programs(1) - 1)
    def _():
        sout_ref[...] = s_ref[...]


def _rwkv_recurrence(rw_t, s0, rk_t, lnw_t, lnb_t, *, tb):
    t_len, _, _, lanes = rw_t.shape
    return pl.pallas_call(
        _rec_kernel,
        grid=(lanes // LANE, t_len // tb),
        in_specs=[pl.BlockSpec((tb, 6, HEAD_DIM, LANE), lambda l, t: (t, 0, 0, l)),
                  pl.BlockSpec((HEAD_DIM, HEAD_DIM, LANE), lambda l, t: (0, 0, l)),
                  pl.BlockSpec((HEAD_DIM, LANE), lambda l, t: (0, l)),
                  pl.BlockSpec((HEAD_DIM, LANE), lambda l, t: (0, l)),
                  pl.BlockSpec((HEAD_DIM, LANE), lambda l, t: (0, l))],
        out_specs=[pl.BlockSpec((tb, HEAD_DIM, LANE), lambda l, t: (t, 0, l)),
                   pl.BlockSpec((HEAD_DIM, HEAD_DIM, LANE), lambda l, t: (0, 0, l))],
        out_shape=[jax.ShapeDtypeStruct((t_len, HEAD_DIM, lanes), F32),
                   jax.ShapeDtypeStruct((HEAD_DIM, HEAD_DIM, lanes), F32)],
        scratch_shapes=[pltpu.VMEM((HEAD_DIM, HEAD_DIM, LANE), F32), pltpu.VMEM((HEAD_DIM, LANE), F32)],
        compiler_params=_params(("parallel", "arbitrary")),
        name="rwkv_recurrence",
    )(rw_t, s0, rk_t, lnw_t, lnb_t)


def _rwkv(rw, n_seq, seq_len, state_bhij, wts, *, tb):
    lanes = n_seq * HEADS
    rw_t = rw.reshape(n_seq, seq_len, 6, HEADS, HEAD_DIM).transpose(1, 2, 4, 0, 3).reshape(seq_len, 6, HEAD_DIM, lanes)
    s0 = state_bhij.transpose(2, 3, 0, 1).reshape(HEAD_DIM, HEAD_DIM, lanes)
    per_lane = lambda p: jnp.tile(p.reshape(HEADS, HEAD_DIM).T, (1, n_seq))
    o_t, s_new = _rwkv_recurrence(rw_t, s0, per_lane(wts["r_k"]), per_lane(wts["ln_x_w"]),
                                  per_lane(wts["ln_x_b"]), tb=tb)
    o = o_t.reshape(seq_len, HEAD_DIM, n_seq, HEADS).transpose(2, 0, 3, 1).reshape(n_seq * seq_len, WIDTH)
    s_out = s_new.reshape(HEAD_DIM, HEAD_DIM, n_seq, HEADS).transpose(2, 3, 0, 1)
    return o, s_out


def _cumsum_kernel(lf_ref, o_ref, carry_ref):
    tb = lf_ref.shape[0]

    @pl.when(pl.program_id(1) == 0)
    def _():
        carry_ref[...] = jnp.zeros_like(carry_ref)

    row = lax.broadcasted_iota(jnp.int32, (tb, tb), 0)
    col = lax.broadcasted_iota(jnp.int32, (tb, tb), 1)
    tri = (col <= row).astype(F32)
    f = jnp.dot(tri, lf_ref[...], precision=lax.Precision.HIGHEST, preferred_element_type=F32) + carry_ref[...]
    o_ref[...] = f
    carry_ref[...] = f[tb - 1:tb, :]


def _cumsum_time(lf3, *, tb=256):
    n_seq, t_len, w = lf3.shape
    return pl.pallas_call(
        _cumsum_kernel,
        grid=(n_seq, t_len // tb),
        in_specs=[pl.BlockSpec((None, tb, w), lambda b, t: (b, t, 0))],
        out_specs=pl.BlockSpec((None, tb, w), lambda b, t: (b, t, 0)),
        out_shape=jax.ShapeDtypeStruct(lf3.shape, F32),
        scratch_shapes=[pltpu.VMEM((1, w), F32)],
        compiler_params=_params(("parallel", "arbitrary")),
        name="logf_cumsum",
    )(lf3)


def _fox_prompt_kernel(q_ref, k_ref, v_ref, fr_ref, o_ref, *, tq, tk):
    qi = pl.program_id(2)
    q = q_ref[...]
    first = lax.broadcasted_iota(jnp.int32, (tq, LANE), 1) < HEAD_DIM
    zero = jnp.zeros_like(q)
    q_heads = (jnp.where(first, q, zero), jnp.where(first, zero, q))
    nt = (((1,), (1,)), ((), ()))

    def block(ki, carry, masked):
        ks = pl.multiple_of(ki * tk, tk)
        k = k_ref[pl.ds(ks, tk), :]
        v = v_ref[pl.ds(ks, tk), :]
        out = []
        for h in range(2):
            m, l, acc = carry[h]
            s = lax.dot_general(q_heads[h], k, nt, preferred_element_type=F32) - fr_ref[h, ki]
            if masked:
                qpos = qi * tq + lax.broadcasted_iota(jnp.int32, (tq, tk), 0)
                kpos = ks + lax.broadcasted_iota(jnp.int32, (tq, tk), 1)
                s = jnp.where(kpos <= qpos, s, -jnp.inf)
            m_new = jnp.maximum(m, jnp.max(s, axis=1, keepdims=True))
            p = jnp.exp(s - m_new)
            alpha = jnp.exp(m - m_new)
            l = alpha * l + jnp.sum(p, axis=1, keepdims=True)
            acc = alpha * acc + jnp.dot(p.astype(BF16), v, preferred_element_type=F32)
            out.append((m_new, l, acc))
        return tuple(out)

    one = (jnp.full((tq, 1), -jnp.inf, F32), jnp.zeros((tq, 1), F32), jnp.zeros((tq, LANE), F32))
    diag = (qi * tq) // tk
    carry = lax.fori_loop(0, diag, lambda ki, c: block(ki, c, False), (one, one))
    (_, l0, acc0), (_, l1, acc1) = block(diag, carry, True)
    o_ref[...] = jnp.where(first, acc0 / l0, acc1 / l1).astype(o_ref.dtype)


def _fox_prompt(q, k, v, f_cum, n_seq, seq_len, *, tq, tk):
    pairs = HEADS // 2
    tk = min(tk, seq_len)
    tq = min(tq, tk)
    seq3 = lambda u: u.reshape(n_seq, seq_len, WIDTH)
    f_row = f_cum[:, :, :HEADS].transpose(0, 2, 1).reshape(n_seq * pairs, 2, seq_len // tk, 1, tk)
    o = pl.pallas_call(
        functools.partial(_fox_prompt_kernel, tq=tq, tk=tk),
        grid=(n_seq, pairs, seq_len // tq),
        in_specs=[pl.BlockSpec((None, tq, LANE), lambda b, h, i: (b, i, h)),
                  pl.BlockSpec((None, seq_len, LANE), lambda b, h, i: (b, 0, h)),
                  pl.BlockSpec((None, seq_len, LANE), lambda b, h, i: (b, 0, h)),
                  pl.BlockSpec((None, 2, seq_len // tk, 1, tk), lambda b, h, i: (b * pairs + h, 0, 0, 0, 0))],
        out_specs=pl.BlockSpec((None, tq, LANE), lambda b, h, i: (b, i, h)),
        out_shape=jax.ShapeDtypeStruct((n_seq, seq_len, WIDTH), BF16),
        compiler_params=_params(("parallel", "parallel", "arbitrary")),
        name="fox_prompt",
    )(seq3(q), seq3(k), seq3(v), f_row)
    return o.reshape(n_seq * seq_len, WIDTH)


PAGE_ROWS = PAGE * HEADS


def _fox_sample_kernel(pt_ref, q_ref, knew_ref, vnew_ref, lfnew_ref, *refs, pages_per_step):
    page_refs = refs[:3 * pages_per_step]
    o_ref, m_ref, l_ref, acc_ref, carry_ref = refs[3 * pages_per_step:]
    p = pl.program_id(1)
    q = q_ref[...]

    @pl.when(p == 0)
    def _():
        m_ref[...] = jnp.sum(q * knew_ref[...], axis=1, keepdims=True)
        l_ref[...] = jnp.ones_like(l_ref)
        acc_ref[...] = vnew_ref[...]
        carry_ref[...] = jnp.broadcast_to(lfnew_ref[...], carry_ref.shape)

    shape = (HEADS, PAGE_ROWS)
    lane = lax.broadcasted_iota(jnp.int32, shape, 1)
    own = (lane % HEADS) == lax.broadcasted_iota(jnp.int32, shape, 0)
    qb = q.astype(BF16)
    for j in range(pages_per_step):
        k_ref, v_ref, lf_ref = page_refs[3 * j:3 * j + 3]
        s = lax.dot_general(qb, k_ref[...].astype(BF16), (((1,), (1,)), ((), ())),
                            preferred_element_type=F32)
        lf = jnp.broadcast_to(lf_ref[...], shape)
        suffix, total = lf, lf
        step = HEADS
        while step < PAGE_ROWS:
            later = pltpu.roll(suffix, PAGE_ROWS - step, axis=1)
            suffix = suffix + jnp.where(lane < PAGE_ROWS - step, later, 0.0)
            total = total + pltpu.roll(total, step, axis=1)
            step *= 2
        carry = carry_ref[...]
        s = jnp.where(own, s + (suffix - lf) + carry, -jnp.inf)
        carry_ref[...] = carry + total
        m_old = m_ref[...]
        m_new = jnp.maximum(m_old, jnp.max(s, axis=1, keepdims=True))
        pr = jnp.exp(s - m_new)
        alpha = jnp.exp(m_old - m_new)
        l_ref[...] = alpha * l_ref[...] + jnp.sum(pr, axis=1, keepdims=True)
        acc_ref[...] = alpha * acc_ref[...] + jnp.dot(pr.astype(BF16), v_ref[...].astype(BF16),
                                                      preferred_element_type=F32)
        m_ref[...] = m_new

    @pl.when(p == pl.num_programs(1) - 1)
    def _():
        o_ref[...] = acc_ref[...] / l_ref[...]


def _fox_sample(q, k_new, v_new, lf_new, cache_k, cache_v, cache_logf, page_table, *, pages_per_step=2):
    n_seq, n_pages = page_table.shape
    n_pool = cache_k.shape[0]
    if n_pages % pages_per_step:
        pages_per_step = 1
    heads = lambda u: u.reshape(n_seq, HEADS, HEAD_DIM)
    lfn = jnp.tile(lf_new[:, :HEADS], (1, PAGE)).reshape(n_seq, 1, PAGE_ROWS)
    ck = cache_k.reshape(n_pool, PAGE_ROWS, HEAD_DIM)
    cv = cache_v.reshape(n_pool, PAGE_ROWS, HEAD_DIM)
    cl = cache_logf.reshape(n_pool, 1, PAGE_ROWS)
    seq3 = lambda b, p, pt: (b, 0, 0)
    page_specs = []
    for j in range(pages_per_step):
        page = lambda b, p, pt, j=j: (pt[b, n_pages - 1 - (p * pages_per_step + j)], 0, 0)
        page_specs += [pl.BlockSpec((None, PAGE_ROWS, HEAD_DIM), page),
                       pl.BlockSpec((None, PAGE_ROWS, HEAD_DIM), page),
                       pl.BlockSpec((None, 1, PAGE_ROWS), page)]
    grid_spec = pltpu.PrefetchScalarGridSpec(
        num_scalar_prefetch=1,
        grid=(n_seq, n_pages // pages_per_step),
        in_specs=[pl.BlockSpec((None, HEADS, HEAD_DIM), seq3),
                  pl.BlockSpec((None, HEADS, HEAD_DIM), seq3),
                  pl.BlockSpec((None, HEADS, HEAD_DIM), seq3),
                  pl.BlockSpec((None, 1, PAGE_ROWS), seq3)] + page_specs,
        out_specs=pl.BlockSpec((None, HEADS, HEAD_DIM), seq3),
        scratch_shapes=[pltpu.VMEM((HEADS, 1), F32), pltpu.VMEM((HEADS, 1), F32),
                        pltpu.VMEM((HEADS, HEAD_DIM), F32), pltpu.VMEM((HEADS, PAGE_ROWS), F32)],
    )
    o = pl.pallas_call(
        functools.partial(_fox_sample_kernel, pages_per_step=pages_per_step),
        grid_spec=grid_spec,
        out_shape=jax.ShapeDtypeStruct((n_seq, HEADS, HEAD_DIM), F32),
        compiler_params=_params(("parallel", "arbitrary")),
        name="fox_sample",
    )(page_table, heads(q), heads(k_new), heads(v_new), lfn, *([ck, cv, cl] * pages_per_step))
    return o.reshape(n_seq, WIDTH)


def _mix_kernel(on_ref, g_ref, bo_ref, sg_ref, x_ref, g1_ref, sc2_ref, sh2_ref, wa_ref, wb_ref, wo_ref,
                l1g_ref, l1b_ref, wq_ref, x1_ref, h2_ref, qp_ref):
    a_out = (on_ref[...] * g_ref[...]).astype(BF16)
    ua = jnp.dot(a_out, wa_ref[...], preferred_element_type=F32)
    ub = jnp.dot(bo_ref[...].astype(BF16), wb_ref[...], preferred_element_type=F32)
    mixed = sg_ref[:, 0:D_MODEL] * ua + sg_ref[:, D_MODEL:2 * D_MODEL] * ub
    mo = jnp.dot(mixed.astype(BF16), wo_ref[...], preferred_element_type=F32)
    x1 = _layer_norm(DN_ALPHA * x_ref[...] + g1_ref[...] * mo, l1g_ref[...], l1b_ref[...])
    x1_ref[...] = x1
    h2 = (x1 * (1.0 + sc2_ref[...]) + sh2_ref[...]).astype(BF16)
    h2_ref[...] = h2
    qp_ref[...] = jnp.dot(h2, wq_ref[...], preferred_element_type=F32).astype(qp_ref.dtype)


def _mix(on, g, bo, sg, x2, g1, sc2, sh2, wts, *, seq_len, per_token, tm):
    n = x2.shape[0]
    tiles_per_seq = 1 if per_token else seq_len // tm
    if per_token:
        mod_spec = pl.BlockSpec((tm, D_MODEL), lambda i: (i, 0))
    else:
        mod_spec = pl.BlockSpec((None, 1, D_MODEL), lambda i: (i // tiles_per_seq, 0, 0))
    row = lambda w: pl.BlockSpec((tm, w), lambda i: (i, 0))
    res = lambda shape: pl.BlockSpec(shape, lambda i: (0, 0), pipeline_mode=pl.Buffered(1))
    nq = PEER_HEADS * 2 * PEER_KEYS
    return pl.pallas_call(
        _mix_kernel,
        grid=(n // tm,),
        in_specs=[row(WIDTH), row(WIDTH), row(WIDTH), row(2 * D_MODEL), row(D_MODEL),
                  mod_spec, mod_spec, mod_spec,
                  res((WIDTH, D_MODEL)), res((WIDTH, D_MODEL)), res((D_MODEL, D_MODEL)),
                  res((1, D_MODEL)), res((1, D_MODEL)), res((D_MODEL, nq))],
        out_specs=[row(D_MODEL), row(D_MODEL), row(nq)],
        out_shape=[jax.ShapeDtypeStruct((n, D_MODEL), F32), jax.ShapeDtypeStruct((n, D_MODEL), BF16),
                   jax.ShapeDtypeStruct((n, nq), BF16)],
        compiler_params=_params(("parallel",)),
        name="mix_ln1_peerq",
    )(on, g, bo, sg, x2, g1, sc2, sh2, wts["w_up_a"], wts["w_up_b"], wts["w_o"], wts["ln1_g"], wts["ln1_b"],
      wts["peer_wq"])


def _top_rows(s, count, payload=None):
    rows = s.shape[0]
    rid = lax.broadcasted_iota(jnp.int32, s.shape, 0).astype(F32)
    vals, pays = [], []
    for _ in range(count):
        m = jnp.max(s, axis=0, keepdims=True)
        am = jnp.min(jnp.where(s == m, rid, float(rows)), axis=0, keepdims=True)
        hit = rid == am
        vals.append(m)
        pays.append(am if payload is None else jnp.max(jnp.where(hit, payload, -1.0), axis=0, keepdims=True))
        s = jnp.where(hit, -jnp.inf, s)
    return jnp.concatenate(vals, axis=0), jnp.concatenate(pays, axis=0)


_PAIR_COUNTS = [PEER_TOPK // (a + 1) for a in range(PEER_TOPK)]
_PAIR_PAD = -sum(_PAIR_COUNTS) % 8


def _peer_topk_kernel(qp_ref, k1_ref, k2_ref, idx_ref, gate_ref):
    tm = qp_ref.shape[0]
    nt = (((1,), (1,)), ((), ()))
    idx_rows, gate_rows = [], []
    for h in range(PEER_HEADS):
        q1 = qp_ref[:, (2 * h) * PEER_KEYS:(2 * h + 1) * PEER_KEYS]
        q2 = qp_ref[:, (2 * h + 1) * PEER_KEYS:(2 * h + 2) * PEER_KEYS]
        s1 = lax.dot_general(k1_ref[...], q1, nt, preferred_element_type=F32)
        s2 = lax.dot_general(k2_ref[...], q2, nt, preferred_element_type=F32)
        t1, i1 = _top_rows(s1, PEER_TOPK)
        t2, i2 = _top_rows(s2, PEER_TOPK)
        cand_s = [t1[a:a + 1, :] + t2[0:nb, :] for a, nb in enumerate(_PAIR_COUNTS)]
        cand_i = [i1[a:a + 1, :] * PEER_KEYS + i2[0:nb, :] for a, nb in enumerate(_PAIR_COUNTS)]
        if _PAIR_PAD:
            cand_s.append(jnp.full((_PAIR_PAD, tm), -jnp.inf, F32))
            cand_i.append(jnp.zeros((_PAIR_PAD, tm), F32))
        top_s, top_i = _top_rows(jnp.concatenate(cand_s, axis=0), PEER_TOPK, jnp.concatenate(cand_i, axis=0))
        e = jnp.exp(top_s - top_s[0:1, :])
        idx_rows.append(top_i)
        gate_rows.append(e / jnp.sum(e, axis=0, keepdims=True))
    idx_ref[...] = jnp.concatenate(idx_rows, axis=0).T.astype(jnp.int32)
    gate_ref[...] = jnp.concatenate(gate_rows, axis=0).T


def _peer_topk(qp, keys1_bf, keys2_bf, *, tm):
    n = qp.shape[0]
    nq = qp.shape[1]
    return pl.pallas_call(
        _peer_topk_kernel,
        grid=(n // tm,),
        in_specs=[pl.BlockSpec((tm, nq), lambda i: (i, 0)),
                  pl.BlockSpec((PEER_KEYS, PEER_KEYS), lambda i: (0, 0)),
                  pl.BlockSpec((PEER_KEYS, PEER_KEYS), lambda i: (0, 0))],
        out_specs=[pl.BlockSpec((tm, PEER_SLOTS), lambda i: (i, 0)), pl.BlockSpec((tm, PEER_SLOTS), lambda i: (i, 0))],
        out_shape=[jax.ShapeDtypeStruct((n, PEER_SLOTS), jnp.int32), jax.ShapeDtypeStruct((n, PEER_SLOTS), F32)],
        compiler_params=_params(("parallel",)),
        name="peer_topk",
    )(qp, keys1_bf, keys2_bf)


def _peer_dense_kernel(h2_ref, gt_ref, u_ref, v_ref, x1_ref, g2_ref, l2g_ref, l2b_ref, y_ref, acc_ref):
    j = pl.program_id(1)

    @pl.when(j == 0)
    def _():
        acc_ref[...] = jnp.zeros_like(acc_ref)

    act = lax.dot_general(h2_ref[...], u_ref[...], (((1,), (1,)), ((), ())), preferred_element_type=F32)
    pw = (gt_ref[...] * jax.nn.gelu(act)).astype(BF16)
    acc_ref[...] += jnp.dot(pw, v_ref[...], preferred_element_type=F32)

    @pl.when(j == pl.num_programs(1) - 1)
    def _():
        y_ref[...] = _layer_norm(DN_ALPHA * x1_ref[...] + g2_ref[...] * acc_ref[...], l2g_ref[...], l2b_ref[...])


def _peer_dense(h2, gates, gate_row0, u_bf, v_bf, x1, g2, wts, *, seq_len, per_token, tm, te):
    n = h2.shape[0]
    tiles_per_seq = 1 if per_token else seq_len // tm
    gate_blk0 = gate_row0 // tm
    if per_token:
        mod_spec = pl.BlockSpec((tm, D_MODEL), lambda i, j: (i, 0))
    else:
        mod_spec = pl.BlockSpec((None, 1, D_MODEL), lambda i, j: (i // tiles_per_seq, 0, 0))
    return pl.pallas_call(
        _peer_dense_kernel,
        grid=(n // tm, PEER_EXPERTS // te),
        in_specs=[pl.BlockSpec((tm, D_MODEL), lambda i, j: (i, 0)),
                  pl.BlockSpec((tm, te), lambda i, j: (gate_blk0 + i, j)),
                  pl.BlockSpec((te, D_MODEL), lambda i, j: (j, 0)),
                  pl.BlockSpec((te, D_MODEL), lambda i, j: (j, 0)),
                  pl.BlockSpec((tm, D_MODEL), lambda i, j: (i, 0)),
                  mod_spec,
                  pl.BlockSpec((1, D_MODEL), lambda i, j: (0, 0)),
                  pl.BlockSpec((1, D_MODEL), lambda i, j: (0, 0))],
        out_specs=pl.BlockSpec((tm, D_MODEL), lambda i, j: (i, 0)),
        out_shape=jax.ShapeDtypeStruct((n, D_MODEL), F32),
        scratch_shapes=[pltpu.VMEM((tm, D_MODEL), F32)],
        compiler_params=_params(("parallel", "arbitrary")),
        name="peer_dense_ln2",
    )(h2, gates, u_bf, v_bf, x1, g2, wts["ln2_g"], wts["ln2_b"])


SC_WORKERS = 32
SC_LANES = 16
SC_CHUNK = 8


def _scatter_gates(idx, gate):
    n = idx.shape[0]
    per_worker = n // SC_WORKERS
    assert n % (SC_WORKERS * SC_CHUNK) == 0
    groups = PEER_SLOTS // SC_LANES
    mesh = plsc.VectorSubcoreMesh(core_axis_name="c", subcore_axis_name="s")

    def body(idx_hbm, gate_hbm, out_hbm, idx_v, gate_v, row0, row1, sem0, sem1):
        worker = lax.axis_index("s") * 2 + lax.axis_index("c")
        base = worker * per_worker
        zeros = jnp.zeros((SC_LANES,), F32)
        rows = (row0, row1)
        sems = (sem0, sem1)

        @pl.loop(0, PEER_EXPERTS, step=SC_LANES)
        def _(i):
            row0[pl.ds(i, SC_LANES)] = zeros
            row1[pl.ds(i, SC_LANES)] = zeros

        @pl.loop(0, per_worker, step=SC_CHUNK)
        def _(c0):
            t0 = base + c0
            pltpu.sync_copy(idx_hbm.at[pl.ds(t0, SC_CHUNK)], idx_v)
            pltpu.sync_copy(gate_hbm.at[pl.ds(t0, SC_CHUNK)], gate_v)
            copies = [None] * SC_CHUNK

            def finish(j):
                copies[j].wait()
                for m in range(groups):
                    plsc.store_scatter(rows[j % 2], [idx_v[j, pl.ds(SC_LANES * m, SC_LANES)]], zeros)

            for j in range(SC_CHUNK):
                if j >= 2:
                    finish(j - 2)
                for m in range(groups):
                    plsc.addupdate_scatter(rows[j % 2], [idx_v[j, pl.ds(SC_LANES * m, SC_LANES)]],
                                           gate_v[j, pl.ds(SC_LANES * m, SC_LANES)])
                copies[j] = pltpu.make_async_copy(rows[j % 2], out_hbm.at[t0 + j], sems[j % 2])
                copies[j].start()
            finish(SC_CHUNK - 2)
            finish(SC_CHUNK - 1)

    return pl.kernel(
        body, mesh=mesh,
        out_type=jax.ShapeDtypeStruct((n, PEER_EXPERTS), F32),
        scratch_types=[pltpu.VMEM((SC_CHUNK, PEER_SLOTS), jnp.int32), pltpu.VMEM((SC_CHUNK, PEER_SLOTS), F32),
                       pltpu.VMEM((PEER_EXPERTS,), F32), pltpu.VMEM((PEER_EXPERTS,), F32),
                       pltpu.SemaphoreType.DMA, pltpu.SemaphoreType.DMA],
        compiler_params=pltpu.CompilerParams(needs_layout_passes=False),
        name="peer_gate_scatter",
    )(idx, gate)


TOPK_TILE = 256


def _layer_front(x, mods, shift_prev, wkv_prev, attend, wts, *, per_token):
    n_seq, seq_len, _ = x.shape
    n = n_seq * seq_len
    x2 = x.reshape(n, D_MODEL)
    if per_token:
        tm_in = tm_mix = n
        sh1, sc1, g1, sh2, sc2, g2 = mods
        prev = shift_prev
    else:
        tm_in, tm_mix = 256, 256
        sh1, sc1, g1, sh2, sc2, g2 = (m[:, None, :] for m in mods)
        prev = shift_prev[:, None, :]
    rw, g, q, k, v, kb, vb, lf, sg, shift = _inproj(x2, sc1, sh1, prev, wts, seq_len=seq_len,
                                                    per_token=per_token, tm=tm_in)
    on, wkv_new = _rwkv(rw, n_seq, seq_len, wkv_prev, wts, tb=min(32, seq_len))
    bo = attend(q, k, v, kb, vb, lf)
    x1, h2, qp = _mix(on, g, bo, sg, x2, g1, sc2, sh2, wts, seq_len=seq_len, per_token=per_token, tm=tm_mix)
    qp = jnp.pad(qp, ((0, -n % TOPK_TILE), (0, 0)))
    idx, gate = _peer_topk(qp, wts["peer_keys1"], wts["peer_keys2"], tm=TOPK_TILE)
    state = (x1, h2, g2)
    outs = (k.reshape(n_seq, seq_len, HEADS, HEAD_DIM), v.reshape(n_seq, seq_len, HEADS, HEAD_DIM),
            lf[:, :HEADS].reshape(n_seq, seq_len, HEADS), wkv_new, shift.reshape(n_seq, SHIFT_COLS))
    return idx, gate, state, outs


def _layer_back(state, gates, gate_row0, wts, *, n_seq, seq_len, per_token):
    x1, h2, g2 = state
    tm = n_seq * seq_len if per_token else min(1024, seq_len)
    y = _peer_dense(h2, gates, gate_row0, wts["peer_u"], wts["peer_v"], x1, g2, wts, seq_len=seq_len,
                    per_token=per_token, tm=tm, te=512)
    return y.reshape(n_seq, seq_len, D_MODEL)


def _layer_weights(l, w_in, mu_shift, w0, w2_decay, a0, a2_iclr, g2_gate, k_k, k_a, r_k, ln_x_w, ln_x_b,
                   b_forget, w_up_a, w_up_b, w_o, ln1_g, ln1_b, peer_wq, peer_keys1, peer_keys2, peer_u,
                   peer_v, ln2_g, ln2_b):
    wi = w_in[l]
    c = SHIFT_COLS + 3 * WIDTH
    w_in_pad = jnp.concatenate(
        [wi[:, :c], jnp.pad(wi[:, c:c + HEADS], ((0, 0), (0, FORGET_PAD - HEADS))), wi[:, c + HEADS:]], axis=1)
    row = lambda p: p[l].reshape(1, -1)
    zeros = jnp.zeros((LANE - DECAY_RANK, WIDTH), F32)
    return {
        "w_in": w_in_pad.astype(BF16), "mu": row(mu_shift), "w0": row(w0),
        "w2": jnp.concatenate([w2_decay[l], zeros], axis=0).astype(BF16),
        "a0": row(a0),
        "a2": jnp.concatenate([zeros, a2_iclr[l]], axis=0).astype(BF16),
        "g2": g2_gate[l].astype(BF16), "k_k": row(k_k), "k_a": row(k_a),
        "b_forget": jnp.pad(b_forget[l], (0, FORGET_PAD - HEADS)).reshape(1, FORGET_PAD),
        "r_k": r_k[l].reshape(-1), "ln_x_w": ln_x_w[l], "ln_x_b": ln_x_b[l],
        "w_up_a": w_up_a[l].astype(BF16), "w_up_b": w_up_b[l].astype(BF16), "w_o": w_o[l].astype(BF16),
        "ln1_g": row(ln1_g), "ln1_b": row(ln1_b), "peer_wq": peer_wq[l].astype(BF16),
        "peer_keys1": peer_keys1[l].astype(BF16), "peer_keys2": peer_keys2[l].astype(BF16),
        "peer_u": peer_u[l].astype(BF16), "peer_v": peer_v[l].astype(BF16),
        "ln2_g": row(ln2_g), "ln2_b": row(ln2_b),
    }


def kernel(x_prompt, x_sample, cache_k, cache_v, cache_logf, state_wkv, state_shift, page_table, c_prompt, c_sample, w_ada, b_ada, w_in, mu_shift, w0, w2_decay, a0, a2_iclr, g2_gate, k_k, k_a, r_k, ln_x_w, ln_x_b, b_forget, w_up_a, w_up_b, w_o, ln1_g, ln1_b, peer_wq, peer_keys1, peer_keys2, peer_u, peer_v, ln2_g, ln2_b):
    depth = w_in.shape[0]
    n_p, t_p, _ = x_prompt.shape
    n_s, t_s, _ = x_sample.shape
    assert t_s == 1, "the sample group is served one new token per sequence"
    yp, ys = x_prompt, x_sample
    outs_p, outs_s = [], []
    c_all = jnp.concatenate([c_prompt, c_sample], axis=0)
    for l in range(depth):
        wts = _layer_weights(l, w_in, mu_shift, w0, w2_decay, a0, a2_iclr, g2_gate, k_k, k_a, r_k, ln_x_w,
                             ln_x_b, b_forget, w_up_a, w_up_b, w_o, ln1_g, ln1_b, peer_wq, peer_keys1,
                             peer_keys2, peer_u, peer_v, ln2_g, ln2_b)
        mod = _ada(c_all, w_ada[l].astype(BF16), b_ada[l].reshape(1, -1))
        mods_p = tuple(mod[:n_p, i * D_MODEL:(i + 1) * D_MODEL] for i in range(6))
        mods_s = tuple(mod[n_p:, i * D_MODEL:(i + 1) * D_MODEL] for i in range(6))

        def attend_prompt(q, k, v, kb, vb, lf):
            f_cum = _cumsum_time(lf.reshape(n_p, t_p, FORGET_PAD))
            return _fox_prompt(q, kb, vb, f_cum, n_p, t_p, tq=256, tk=512)

        def attend_sample(q, k, v, kb, vb, lf, l=l):
            return _fox_sample(q.astype(F32), k, v, lf, cache_k[l], cache_v[l], cache_logf[l], page_table)

        shift0 = jnp.zeros((n_p, SHIFT_COLS), F32)
        wkv0 = jnp.zeros((n_p, HEADS, HEAD_DIM, HEAD_DIM), F32)
        idx_p, gate_p, state_p, out_p = _layer_front(yp, mods_p, shift0, wkv0, attend_prompt, wts, per_token=False)
        idx_s, gate_s, state_s, out_s = _layer_front(ys, mods_s, state_shift[l], state_wkv[l], attend_sample, wts,
                                                     per_token=True)
        gates = _scatter_gates(jnp.concatenate([idx_p, idx_s], axis=0), jnp.concatenate([gate_p, gate_s], axis=0))
        yp = _layer_back(state_p, gates, 0, wts, n_seq=n_p, seq_len=t_p, per_token=False)
        ys = _layer_back(state_s, gates, idx_p.shape[0], wts, n_seq=n_s, seq_len=t_s, per_token=True)
        outs_p.append(out_p)
        outs_s.append(out_s)
    stack = lambda outs, i, dt: jnp.stack([o[i] for o in outs]).astype(dt)
    return (yp, ys,
            stack(outs_p, 0, cache_k.dtype), stack(outs_p, 1, cache_v.dtype), stack(outs_p, 2, cache_logf.dtype),
            stack(outs_p, 3, state_wkv.dtype), stack(outs_p, 4, state_shift.dtype),
            stack(outs_s, 0, cache_k.dtype), stack(outs_s, 1, cache_v.dtype), stack(outs_s, 2, cache_logf.dtype),
            stack(outs_s, 3, state_wkv.dtype), stack(outs_s, 4, state_shift.dtype))
```
